```python
import math, functools
import jax, jax.numpy as jnp
from jax import lax
import numpy as np

D_MODEL = 1024
BATCH = 2
SEQ = 8192
DEPTH = 4
DEC_BATCH = 128
DEC_SEQ = 1
PAST_LEN = 2048
PAGE_SIZE = 128

N_A = (DEPTH + 1) // 2
N_C = DEPTH // 2

SSD_INNER = D_MODEL
SSD_HEADDIM = 64
SSD_HEADS = SSD_INNER // SSD_HEADDIM
SSD_GROUPS = 2
SSD_HPG = SSD_HEADS // SSD_GROUPS
SSD_STATE = 128
SSD_CONV = 4
SSD_CHUNK = 128
SSD_CONV_CH = SSD_INNER + 2 * SSD_GROUPS * SSD_STATE

DIFF_HEADS = 8
DIFF_DK = 64
DIFF_DV = 2 * DIFF_DK
DIFF_QK = DIFF_HEADS * 2 * DIFF_DK
DIFF_WIDTH = DIFF_HEADS * DIFF_DV
ROT_DIM = DIFF_DK // 4
ROPE_THETA = 500000.0
Q_BLOCK = 128

CONF_WIDTH = 2 * D_MODEL
CONF_CONV_W = 31

N_MEM = 256
X_HEADS = 4
X_HEADDIM = 128
X_WIDTH = X_HEADS * X_HEADDIM

IN_A = SSD_INNER + SSD_CONV_CH + SSD_HEADS + 2 * DIFF_QK + 2 * DIFF_WIDTH + 2 * X_WIDTH
OUT_A = SSD_INNER + DIFF_WIDTH + X_WIDTH
IN_C = 3 * CONF_WIDTH + 2 * X_WIDTH
OUT_C = CONF_WIDTH + X_WIDTH
EPS = 1e-6

kernel_name = "hybrid_ssd_diffattn_conformer_decoder_step"


def _rmsnorm(x, g):
    xf = x.astype(jnp.float32)
    y = xf * lax.rsqrt(jnp.mean(xf * xf, axis=-1, keepdims=True) + EPS)
    return (y * g.astype(jnp.float32)).astype(x.dtype)


def _layernorm(x, g, b):
    xf = x.astype(jnp.float32)
    xc = xf - jnp.mean(xf, axis=-1, keepdims=True)
    y = xc * lax.rsqrt(jnp.mean(xc * xc, axis=-1, keepdims=True) + EPS)
    return (y * g.astype(jnp.float32) + b.astype(jnp.float32)).astype(x.dtype)


def _causal_dwconv(prev, x, w, b):
    k = w.shape[0]
    xp = jnp.concatenate([prev.astype(x.dtype), x], axis=1)
    y = lax.conv_general_dilated(xp, w[:, None, :].astype(x.dtype), window_strides=(1,), padding="VALID",
                                 dimension_numbers=("NWC", "WIO", "NWC"), feature_group_count=x.shape[-1])
    return y + b.astype(x.dtype), xp[:, xp.shape[1] - (k - 1):]


def _partial_rotary(x, pos):
    half = ROT_DIM // 2
    inv_freq = ROPE_THETA ** (-jnp.arange(0, ROT_DIM, 2, dtype=jnp.float32) / ROT_DIM)
    ang = pos.astype(jnp.float32)[:, None] * inv_freq[None, :]
    cos = jnp.cos(ang)[:, None, None, :]
    sin = jnp.sin(ang)[:, None, None, :]
    xr = x[..., :ROT_DIM].astype(jnp.float32)
    x1, x2 = xr[..., :half], xr[..., half:]
    rot = jnp.concatenate([x1 * cos - x2 * sin, x2 * cos + x1 * sin], axis=-1)
    return jnp.concatenate([rot.astype(x.dtype), x[..., ROT_DIM:]], axis=-1)


def _ssd_scan(x, dt, a, bm, cm, h0):
    b_, L, G, R, P = x.shape
    N = bm.shape[-1]
    q = math.gcd(L, SSD_CHUNK)
    c = L // q
    x = x.reshape(b_, c, q, G, R, P)
    dt = dt.reshape(b_, c, q, G, R)
    bm = bm.reshape(b_, c, q, G, N)
    cm = cm.reshape(b_, c, q, G, N)
    acum = jnp.moveaxis(jnp.cumsum(dt * a, axis=2), 2, -1)
    seg = acum[..., :, None] - acum[..., None, :]
    tri = jnp.tril(jnp.ones((q, q), dtype=bool))
    decay = jnp.where(tri, jnp.exp(jnp.where(tri, seg, 0.0)), 0.0)
    xdt = x * dt[..., None]
    scores = jnp.einsum("bcign,bcjgn->bcgij", cm, bm)
    y_diag = jnp.einsum("bcgrij,bcjgrp->bcigrp", scores[:, :, :, None] * decay, xdt)
    w_end = jnp.exp(acum[..., -1:] - acum)
    s_loc = jnp.einsum("bcjgn,bcgrj,bcjgrp->bcgrpn", bm, w_end, xdt)
    chunk_decay = jnp.exp(acum[..., -1])

    def step(h, inp):
        s, d = inp
        return h * d[..., None, None] + s, h

    h_last, h_in = lax.scan(step, h0, (jnp.moveaxis(s_loc, 1, 0), jnp.moveaxis(chunk_decay, 1, 0)))
    h_in = jnp.moveaxis(h_in, 0, 1)
    y_off = jnp.einsum("bcign,bcgri,bcgrpn->bcigrp", cm, jnp.exp(acum), h_in)
    return (y_diag + y_off).reshape(b_, L, G, R, P), h_last


def _ssd_branch(z, xbc, dt_raw, conv_prev, h_prev, conv_w, conv_b, dt_bias, a_log, d_skip, norm_g):
    b_, L, _ = z.shape
    xbc, conv_new = _causal_dwconv(conv_prev, xbc, conv_w, conv_b)
    xbc = jax.nn.silu(xbc)
    xs, bm, cm = jnp.split(xbc, [SSD_INNER, SSD_INNER + SSD_GROUPS * SSD_STATE], axis=-1)
    xs = xs.reshape(b_, L, SSD_GROUPS, SSD_HPG, SSD_HEADDIM).astype(jnp.float32)
    bm = bm.reshape(b_, L, SSD_GROUPS, SSD_STATE).astype(jnp.float32)
    cm = cm.reshape(b_, L, SSD_GROUPS, SSD_STATE).astype(jnp.float32)
    dt = jax.nn.softplus(dt_raw.astype(jnp.float32) + dt_bias.astype(jnp.float32)).reshape(b_, L, SSD_GROUPS, SSD_HPG)
    a = -jnp.exp(a_log.astype(jnp.float32)).reshape(SSD_GROUPS, SSD_HPG)
    y, h_last = _ssd_scan(xs, dt, a, bm, cm, h_prev.astype(jnp.float32))
    y = y + d_skip.astype(jnp.float32).reshape(SSD_GROUPS, SSD_HPG)[:, :, None] * xs
    y = y.reshape(b_, L, SSD_INNER) * jax.nn.silu(z.astype(jnp.float32))
    return _rmsnorm(y, norm_g).astype(z.dtype), conv_new, h_last.astype(h_prev.dtype)


def _diff_attn_prompt(q, k, v, lam):
    b_, S = q.shape[:2]
    qb = math.gcd(S, Q_BLOCK)
    nb = S // qb
    qs = jnp.moveaxis(q.reshape(b_, nb, qb, DIFF_HEADS, 2, DIFF_DK), 1, 0)
    kpos = jnp.arange(S)
    scale = DIFF_DK ** -0.5

    def block(args):
        qi, i = args
        s = jnp.einsum("bqhcd,bkhcd->bhcqk", qi, k, preferred_element_type=jnp.float32) * scale
        qpos = i * qb + jnp.arange(qb)
        s = jnp.where(kpos[None, :] <= qpos[:, None], s, -jnp.inf)
        p = jax.nn.softmax(s, axis=-1)
        w = (p[:, :, 0] - lam * p[:, :, 1]).astype(v.dtype)
        return jnp.einsum("bhqk,bkhd->bqhd", w, v, preferred_element_type=jnp.float32).astype(v.dtype)

    out = lax.map(block, (qs, jnp.arange(nb)))
    return jnp.moveaxis(out, 0, 1).reshape(b_, S, DIFF_HEADS, DIFF_DV)


def _diff_attn_sample(q, k, v, lam, k_past, v_past):
    L = q.shape[1]
    P = k_past.shape[1]
    scale = DIFF_DK ** -0.5
    s_past = jnp.einsum("bqhcd,bkhcd->bhcqk", q, k_past, preferred_element_type=jnp.float32) * scale
    s_new = jnp.einsum("bqhcd,bkhcd->bhcqk", q, k, preferred_element_type=jnp.float32) * scale
    s_new = jnp.where(jnp.tril(jnp.ones((L, L), dtype=bool)), s_new, -jnp.inf)
    p = jax.nn.softmax(jnp.concatenate([s_past, s_new], axis=-1), axis=-1)
    w = (p[:, :, 0] - lam * p[:, :, 1]).astype(v.dtype)
    o = (jnp.einsum("bhqk,bkhd->bqhd", w[..., :P], v_past, preferred_element_type=jnp.float32)
         + jnp.einsum("bhqk,bkhd->bqhd", w[..., P:], v, preferred_element_type=jnp.float32))
    return o.astype(v.dtype)


def _mem_kv(mem, mem_g, wk, wv):
    b_, m, _ = mem.shape
    hm = _rmsnorm(mem, mem_g)
    return ((hm @ wk).reshape(b_, m, X_HEADS, X_HEADDIM), (hm @ wv).reshape(b_, m, X_HEADS, X_HEADDIM))


def _cross_attn(xq, mk, mv):
    b_, L, _ = xq.shape
    q = xq.reshape(b_, L, X_HEADS, X_HEADDIM)
    s = jnp.einsum("bqhd,bkhd->bhqk", q, mk, preferred_element_type=jnp.float32) * (X_HEADDIM ** -0.5)
    p = jax.nn.softmax(s, axis=-1).astype(mv.dtype)
    return jnp.einsum("bhqk,bkhd->bqhd", p, mv).reshape(b_, L, X_WIDTH)


def _even_mixer(x, pos, lam_init, norm_g, w_in, w_out, conv_w, conv_b, dt_bias, a_log, d_skip, ssm_g,
                lam_vecs, subln_g, conv_prev, h_prev, attend, mk, mv):
    b_, L, _ = x.shape
    h = _rmsnorm(x, norm_g)
    proj = h @ w_in
    cuts = np.cumsum([SSD_INNER, SSD_CONV_CH, SSD_HEADS, DIFF_QK, DIFF_QK, DIFF_WIDTH, DIFF_WIDTH, X_WIDTH]).tolist()
    z, xbc, dt_raw, q, k, v, g, xq, xg = jnp.split(proj, cuts, axis=-1)
    y_ssd, conv_new, h_new = _ssd_branch(z, xbc, dt_raw, conv_prev, h_prev, conv_w, conv_b, dt_bias, a_log, d_skip, ssm_g)
    q = _partial_rotary(q.reshape(b_, L, DIFF_HEADS, 2, DIFF_DK), pos)
    k = _partial_rotary(k.reshape(b_, L, DIFF_HEADS, 2, DIFF_DK), pos)
    v = v.reshape(b_, L, DIFF_HEADS, DIFF_DV)
    lv = lam_vecs.astype(jnp.float32)
    lam = jnp.exp(jnp.sum(lv[0] * lv[1])) - jnp.exp(jnp.sum(lv[2] * lv[3])) + lam_init
    o = attend(q, k, v, lam)
    o = _rmsnorm(o, subln_g) * (1.0 - lam_init)
    y_diff = o.reshape(b_, L, DIFF_WIDTH) * jax.nn.silu(g)
    y_x = _cross_attn(xq, mk, mv) * jax.nn.silu(xg)
    out = jnp.concatenate([y_ssd, y_diff, y_x], axis=-1) @ w_out
    return x + out, k, v, conv_new, h_new


def _odd_mixer(x, norm_g, w_in, w_out, dw_w, dw_b, ln_g, ln_b, conv_prev, mk, mv):
    h = _rmsnorm(x, norm_g)
    proj = h @ w_in
    cuts = np.cumsum([CONF_WIDTH, CONF_WIDTH, CONF_WIDTH, X_WIDTH]).tolist()
    a, a_gate, g, xq, xg = jnp.split(proj, cuts, axis=-1)
    u = a * jax.nn.sigmoid(a_gate)
    c, conv_new = _causal_dwconv(conv_prev, u, dw_w, dw_b)
    c = jax.nn.silu(_layernorm(c, ln_g, ln_b))
    y_conf = c * jax.nn.silu(g)
    y_x = _cross_attn(xq, mk, mv) * jax.nn.silu(xg)
    out = jnp.concatenate([y_conf, y_x], axis=-1) @ w_out
    return x + out, conv_new


def setup_inputs(seed: int = 0) -> dict:
    key = jax.random.key(seed)
    ks = jax.random.split(key, 40)
    n_pages = PAST_LEN // PAGE_SIZE
    n_used = DEC_BATCH * n_pages
    n_pool = n_used + n_used // 4

    def nrm(k, shape, scale):
        return scale * jax.random.normal(k, shape, jnp.float32)

    dt0 = jnp.exp(jax.random.uniform(ks[20], (N_A, SSD_HEADS), jnp.float32, math.log(1e-3), math.log(1e-1)))
    page_table = jax.random.permutation(ks[30], n_pool)[:n_used].reshape(DEC_BATCH, n_pages).astype(jnp.int32)
    return {
        "x_prompt": nrm(ks[0], (BATCH, SEQ, D_MODEL), 1.0),
        "x_sample": nrm(ks[1], (DEC_BATCH, DEC_SEQ, D_MODEL), 1.0),
        "cache_attn_k": nrm(ks[2], (N_A, n_pool, PAGE_SIZE, DIFF_HEADS, 2, DIFF_DK), 1.0),
        "cache_attn_v": nrm(ks[3], (N_A, n_pool, PAGE_SIZE, DIFF_HEADS, DIFF_DV), 1.0),
        "cache_mem_k": nrm(ks[4], (DEPTH, DEC_BATCH, N_MEM, X_HEADS, X_HEADDIM), 1.0),
        "cache_mem_v": nrm(ks[5], (DEPTH, DEC_BATCH, N_MEM, X_HEADS, X_HEADDIM), 1.0),
        "state_ssm": nrm(ks[6], (N_A, DEC_BATCH, SSD_GROUPS, SSD_HPG, SSD_HEADDIM, SSD_STATE), 0.1),
        "state_ssm_conv": nrm(ks[7], (N_A, DEC_BATCH, SSD_CONV - 1, SSD_CONV_CH), 1.0),
        "state_conf_conv": nrm(ks[8], (N_C, DEC_BATCH, CONF_CONV_W - 1, CONF_WIDTH), 1.0),
        "page_table": page_table,
        "mem_prompt": nrm(ks[9], (BATCH, N_MEM, D_MODEL), 1.0),
        "norm_a": 1.0 + nrm(ks[10], (N_A, D_MODEL), 0.02),
        "w_in_a": nrm(ks[11], (N_A, D_MODEL, IN_A), D_MODEL ** -0.5),
        "ssm_conv_w": nrm(ks[12], (N_A, SSD_CONV, SSD_CONV_CH), SSD_CONV ** -0.5),
        "ssm_conv_b": nrm(ks[13], (N_A, SSD_CONV_CH), 0.01),
        "ssm_dt_bias": dt0 + jnp.log(-jnp.expm1(-dt0)),
        "ssm_a_log": jnp.log(jax.random.uniform(ks[14], (N_A, SSD_HEADS), jnp.float32, 1.0, 16.0)),
        "ssm_d": 1.0 + nrm(ks[15], (N_A, SSD_HEADS), 0.1),
        "ssm_norm": 1.0 + nrm(ks[16], (N_A, SSD_INNER), 0.02),
        "diff_lambda": nrm(ks[17], (N_A, 4, DIFF_DK), 0.1),
        "diff_subln": 1.0 + nrm(ks[18], (N_A, DIFF_DV), 0.02),
        "w_out_a": nrm(ks[19], (N_A, OUT_A, D_MODEL), OUT_A ** -0.5),
        "norm_c": 1.0 + nrm(ks[21], (N_C, D_MODEL), 0.02),
        "w_in_c": nrm(ks[22], (N_C, D_MODEL, IN_C), D_MODEL ** -0.5),
        "conf_dw_w": nrm(ks[23], (N_C, CONF_CONV_W, CONF_WIDTH), CONF_CONV_W ** -0.5),
        "conf_dw_b": nrm(ks[24], (N_C, CONF_WIDTH), 0.01),
        "conf_ln_g": 1.0 + nrm(ks[25], (N_C, CONF_WIDTH), 0.02),
        "conf_ln_b": nrm(ks[26], (N_C, CONF_WIDTH), 0.01),
        "w_out_c": nrm(ks[27], (N_C, OUT_C, D_MODEL), OUT_C ** -0.5),
        "xattn_wk": nrm(ks[28], (DEPTH, D_MODEL, X_WIDTH), D_MODEL ** -0.5),
        "xattn_wv": nrm(ks[29], (DEPTH, D_MODEL, X_WIDTH), D_MODEL ** -0.5),
        "mem_norm": 1.0 + nrm(ks[31], (D_MODEL,), 0.02),
        "norm_f": 1.0 + nrm(ks[32], (D_MODEL,), 0.02),
    }


def reference(x_prompt, x_sample, cache_attn_k, cache_attn_v, cache_mem_k, cache_mem_v, state_ssm, state_ssm_conv,
              state_conf_conv, page_table, mem_prompt, norm_a, w_in_a, ssm_conv_w, ssm_conv_b, ssm_dt_bias, ssm_a_log,
              ssm_d, ssm_norm, diff_lambda, diff_subln, w_out_a, norm_c, w_in_c, conf_dw_w, conf_dw_b, conf_ln_g,
              conf_ln_b, w_out_c, xattn_wk, xattn_wv, mem_norm, norm_f):
    bp, sp, _ = x_prompt.shape
    bs, ls, _ = x_sample.shape
    past = page_table.shape[1] * PAGE_SIZE
    pos_p = jnp.arange(sp)
    pos_s = past + jnp.arange(ls)
    dtp = x_prompt.dtype
    xp, xs = x_prompt, x_sample
    pk, pv, pssm, pconv, pconf, pmk, pmv = [], [], [], [], [], [], []
    sk, sv, sssm, sconv, sconf = [], [], [], [], []
    for layer in range(DEPTH):
        j = layer // 2
        mk_p, mv_p = _mem_kv(mem_prompt, mem_norm, xattn_wk[layer], xattn_wv[layer])
        pmk.append(mk_p)
        pmv.append(mv_p)
        mk_s, mv_s = cache_mem_k[layer], cache_mem_v[layer]
        if layer % 2 == 0:
            lam_init = 0.8 - 0.6 * math.exp(-0.3 * layer)
            shared = (lam_init, norm_a[j], w_in_a[j], w_out_a[j], ssm_conv_w[j], ssm_conv_b[j], ssm_dt_bias[j],
                      ssm_a_log[j], ssm_d[j], ssm_norm[j], diff_lambda[j], diff_subln[j])
            conv0 = jnp.zeros((bp, SSD_CONV - 1, SSD_CONV_CH), dtp)
            h0 = jnp.zeros((bp, SSD_GROUPS, SSD_HPG, SSD_HEADDIM, SSD_STATE), dtp)
            xp, k_, v_, c_, h_ = _even_mixer(xp, pos_p, *shared, conv0, h0, _diff_attn_prompt, mk_p, mv_p)
            pk.append(k_)
            pv.append(v_)
            pconv.append(c_)
            pssm.append(h_)
            k_past = cache_attn_k[j, page_table].reshape(bs, past, DIFF_HEADS, 2, DIFF_DK)
            v_past = cache_attn_v[j, page_table].reshape(bs, past, DIFF_HEADS, DIFF_DV)
            attend_s = functools.partial(_diff_attn_sample, k_past=k_past, v_past=v_past)
            xs, k_, v_, c_, h_ = _even_mixer(xs, pos_s, *shared, state_ssm_conv[j], state_ssm[j], attend_s, mk_s, mv_s)
            sk.append(k_)
            sv.append(v_)
            sconv.append(c_)
            sssm.append(h_)
        else:
            shared = (norm_c[j], w_in_c[j], w_out_c[j], conf_dw_w[j], conf_dw_b[j], conf_ln_g[j], conf_ln_b[j])
            conv0 = jnp.zeros((bp, CONF_CONV_W - 1, CONF_WIDTH), dtp)
            xp, c_ = _odd_mixer(xp, *shared, conv0, mk_p, mv_p)
            pconf.append(c_)
            xs, c_ = _odd_mixer(xs, *shared, state_conf_conv[j], mk_s, mv_s)
            sconf.append(c_)
    y_prompt = _rmsnorm(xp, norm_f)
    y_sample = _rmsnorm(xs, norm_f)
    return (y_prompt, y_sample, jnp.stack(pk), jnp.stack(pv), jnp.stack(pssm), jnp.stack(pconv), jnp.stack(pconf),
            jnp.stack(pmk), jnp.stack(pmv), jnp.stack(sk), jnp.stack(sv), jnp.stack(sssm), jnp.stack(sconv),
            jnp.stack(sconf))
```

```python
import functools
import math

import jax
import jax.numpy as jnp
import numpy as np
from jax import lax
from jax.experimental import pallas as pl
from jax.experimental.pallas import tpu as pltpu

F32 = jnp.float32
BF16 = jnp.bfloat16

D_MODEL = 1024
DEPTH = 4
PAGE_SIZE = 128
SSD_INNER = 1024
SSD_HEADDIM = 64
SSD_HEADS = 16
SSD_GROUPS = 2
SSD_HPG = 8
SSD_STATE = 128
SSD_CONV = 4
SSD_CHUNK = 128
SSD_CONV_CH = 1536
DIFF_HEADS = 8
DIFF_DK = 64
DIFF_DV = 128
DIFF_QK = 1024
DIFF_WIDTH = 1024
ROT_DIM = 16
ROPE_THETA = 500000.0
CONF_WIDTH = 2048
CONF_CONV_W = 31
N_MEM = 256
X_HEADS = 4
X_HEADDIM = 128
X_WIDTH = 512
EPS = 1e-6

LANES = 128
SUBLANES = 8
VMEM_LIMIT_BYTES = 56 * 1024 * 1024
NEG_BIG = -1e30


def _params(*sem):
    return pltpu.CompilerParams(dimension_semantics=sem, vmem_limit_bytes=VMEM_LIMIT_BYTES)


def _silu(x):
    return x / (1.0 + jnp.exp(-x))


def _split2(x):
    hi = x.astype(BF16)
    lo = (x - hi.astype(F32)).astype(BF16)
    return hi, lo


def _split3(x):
    hi = x.astype(BF16)
    r = x - hi.astype(F32)
    mid = r.astype(BF16)
    lo = (r - mid.astype(F32)).astype(BF16)
    return hi, mid, lo


def _dot(a, b):
    return jnp.dot(a, b, preferred_element_type=F32)


def _dot_nt(a, b):
    return lax.dot_general(a, b, (((1,), (1,)), ((), ())), preferred_element_type=F32)


def _expand(parts, e):
    acc = _dot(parts[0], e)
    for p in parts[1:]:
        acc = acc + _dot(p, e)
    return acc


def _norm_matmul_body(x_ref, g_ref, w_ref, o_ref, h_ref):
    @pl.when(pl.program_id(1) == 0)
    def _():
        x = x_ref[...]
        r = lax.rsqrt(jnp.mean(x * x, axis=-1, keepdims=True) + EPS)
        h_ref[...] = ((x * r) * g_ref[...]).astype(BF16)

    o_ref[...] = _dot(h_ref[...], w_ref[...])


def _norm_matmul(x, g, w, tm, tn):
    m, d = x.shape
    n = w.shape[1]
    return pl.pallas_call(
        _norm_matmul_body,
        grid=(m // tm, n // tn),
        in_specs=[pl.BlockSpec((tm, d), lambda i, j: (i, 0)),
                  pl.BlockSpec((1, d), lambda i, j: (0, 0)),
                  pl.BlockSpec((d, tn), lambda i, j: (0, j))],
        out_specs=pl.BlockSpec((tm, tn), lambda i, j: (i, j)),
        out_shape=jax.ShapeDtypeStruct((m, n), F32),
        scratch_shapes=[pltpu.VMEM((tm, d), BF16)],
        compiler_params=_params("parallel", "arbitrary"),
        name="norm_matmul",
    )(x, g.reshape(1, d), w)


def _out_proj_body(*refs, n_in, final):
    ins = refs[:n_in]
    ws = refs[n_in:2 * n_in]
    x_ref = refs[2 * n_in]
    g_ref = refs[2 * n_in + 1]
    o_ref = refs[2 * n_in + 2]
    acc = x_ref[...]
    for a, w in zip(ins, ws):
        acc = acc + _dot(a[...], w[...])
    if final:
        r = lax.rsqrt(jnp.mean(acc * acc, axis=-1, keepdims=True) + EPS)
        acc = (acc * r) * g_ref[...]
    o_ref[...] = acc


def _out_proj(parts, w, x, norm_f, final, tm):
    m, d = x.shape
    widths = [p.shape[1] for p in parts]
    offs = np.cumsum([0] + widths).tolist()
    ws = [w[offs[i]:offs[i + 1]] for i in range(len(parts))]
    in_specs = ([pl.BlockSpec((tm, wd), lambda i: (i, 0)) for wd in widths]
                + [pl.BlockSpec((wd, d), lambda i: (0, 0)) for wd in widths]
                + [pl.BlockSpec((tm, d), lambda i: (i, 0)), pl.BlockSpec((1, d), lambda i: (0, 0))])
    return pl.pallas_call(
        functools.partial(_out_proj_body, n_in=len(parts), final=final),
        grid=(m // tm,),
        in_specs=in_specs,
        out_specs=pl.BlockSpec((tm, d), lambda i: (i, 0)),
        out_shape=jax.ShapeDtypeStruct((m, d), F32),
        compiler_params=_params("parallel"),
        name="out_proj",
    )(*parts, *ws, x, norm_f.reshape(1, d))


def _rope_tables(pos):
    half = ROT_DIM // 2
    inv_freq = ROPE_THETA ** (-jnp.arange(0, ROT_DIM, 2, dtype=F32) / ROT_DIM)
    ang = pos.astype(F32)[:, None] * inv_freq[None, :]
    cos, sin = jnp.cos(ang), jnp.sin(ang)
    n = pos.shape[0]
    zeros = jnp.zeros((n, DIFF_DK - ROT_DIM), F32)
    z8 = jnp.zeros((n, half), F32)
    c = jnp.concatenate([cos, cos, zeros + 1.0], axis=-1)
    sa = jnp.concatenate([-sin, z8, zeros], axis=-1)
    sb = jnp.concatenate([z8, sin, zeros], axis=-1)
    return (jnp.concatenate([c, c], axis=-1), jnp.concatenate([sa, sa], axis=-1),
            jnp.concatenate([sb, sb], axis=-1))


def _rope_body(q_ref, k_ref, v_ref, c_ref, sa_ref, sb_ref, qb_ref, kr_ref, kb_ref, vb_ref, *, transpose_q):
    c, sa, sb = c_ref[...], sa_ref[...], sb_ref[...]
    half = ROT_DIM // 2

    def rot(x):
        return x * c + pltpu.roll(x, LANES - half, 1) * sa + pltpu.roll(x, half, 1) * sb

    for h in range(DIFF_HEADS):
        sl = slice(h * LANES, (h + 1) * LANES)
        kr = rot(k_ref[:, sl])
        kr_ref[:, sl] = kr
        kb_ref[:, sl] = kr.astype(BF16)
        qs = rot(q_ref[:, sl]) * (DIFF_DK ** -0.5)
        if transpose_q:
            qb_ref[sl, :] = qs.T.astype(BF16)
        else:
            qb_ref[:, sl] = qs.astype(BF16)
    vb_ref[...] = v_ref[...].astype(BF16)


def _rope(proj, tables, s_len, tm, transpose_q=False):
    t = proj.shape[0]
    w = DIFF_QK
    nt = s_len // tm
    tab_spec = pl.BlockSpec((tm, LANES), lambda i: (i % nt, 0))
    row = lambda cb: pl.BlockSpec((tm, w), lambda i: (i, cb))
    if transpose_q:
        assert t == tm
        q_spec, q_shape = pl.BlockSpec((w, tm), lambda i: (0, 0)), (w, t)
    else:
        q_spec, q_shape = row(0), (t, w)
    return pl.pallas_call(
        functools.partial(_rope_body, transpose_q=transpose_q),
        grid=(t // tm,),
        in_specs=[row(1), row(2), row(3), tab_spec, tab_spec, tab_spec],
        out_specs=[q_spec, row(0), row(0), row(0)],
        out_shape=[jax.ShapeDtypeStruct(q_shape, BF16), jax.ShapeDtypeStruct((t, w), F32),
                   jax.ShapeDtypeStruct((t, w), BF16), jax.ShapeDtypeStruct((t, w), BF16)],
        compiler_params=_params("parallel"),
        name="rope",
    )(proj, proj, proj, *tables)


def _diff_lambda(lam_ref, lam_init):
    lv = lam_ref[...]
    a = jnp.sum(lv[0:1] * lv[1:2], axis=-1, keepdims=True)
    b = jnp.sum(lv[2:3] * lv[3:4], axis=-1, keepdims=True)
    return jnp.exp(a) - jnp.exp(b) + lam_init


def _subln_gate(o, sub_g, g, lam_init):
    r = lax.rsqrt(jnp.mean(o * o, axis=-1, keepdims=True) + EPS)
    return ((o * r) * sub_g) * (1.0 - lam_init) * _silu(g)


def _diff_attn_body(qi_ref, ki_ref, q_ref, k_ref, v_ref, g_ref, lam_ref, sg_ref, o_ref,
                    m_ref, l_ref, acc_ref, *, tq, lam_init):
    p = pl.program_id(2)
    qi = qi_ref[p]
    ki = ki_ref[p]

    @pl.when(ki == 0)
    def _():
        m_ref[...] = jnp.full(m_ref.shape, NEG_BIG, F32)
        l_ref[...] = jnp.zeros(l_ref.shape, F32)
        acc_ref[...] = jnp.zeros(acc_ref.shape, F32)

    q = q_ref[...]
    lane = lax.broadcasted_iota(jnp.int32, q.shape, 1)
    zero = jnp.zeros_like(q)
    qc = (jnp.where(lane < DIFF_DK, q, zero), jnp.where(lane >= DIFF_DK, q, zero))
    k = k_ref[...]
    v = v_ref[...]

    def step(masked):
        for c in range(2):
            s = _dot_nt(qc[c], k)
            if masked:
                row = lax.broadcasted_iota(jnp.int32, s.shape, 0)
                col = lax.broadcasted_iota(jnp.int32, s.shape, 1)
                s = jnp.where(col <= row, s, NEG_BIG)
            m_old = m_ref[c]
            m_new = jnp.maximum(m_old, jnp.max(s, axis=-1, keepdims=True))
            alpha = jnp.exp(m_old - m_new)
            pr = jnp.exp(s - m_new)
            l_ref[c] = alpha * l_ref[c] + jnp.sum(pr, axis=-1, keepdims=True)
            acc_ref[c] = alpha * acc_ref[c] + _dot(pr.astype(BF16), v)
            m_ref[c] = m_new

    @pl.when(ki < qi)
    def _():
        step(False)

    @pl.when(ki == qi)
    def _():
        step(True)
        lam = _diff_lambda(lam_ref, lam_init)
        o = acc_ref[0] / l_ref[0] - lam * (acc_ref[1] / l_ref[1])
        o_ref[...] = _subln_gate(o, sg_ref[...], g_ref[...], lam_init).astype(BF16)


def _diff_attn_prompt(qb, kb, vb, proj, lam_vecs, sub_g, lam_init, nb, s_len, tq):
    t = qb.shape[0]
    nq = s_len // tq
    pairs = [(i, j) for i in range(nq) for j in range(i + 1)]
    qi_tab = jnp.asarray([a for a, _ in pairs], jnp.int32)
    ki_tab = jnp.asarray([b for _, b in pairs], jnp.int32)
    g_cb = (SSD_INNER + 2 * DIFF_QK + DIFF_WIDTH) // LANES
    grid_spec = pltpu.PrefetchScalarGridSpec(
        num_scalar_prefetch=2,
        grid=(nb, DIFF_HEADS, len(pairs)),
        in_specs=[pl.BlockSpec((tq, LANES), lambda b, h, p, qt, kt: (b * nq + qt[p], h)),
                  pl.BlockSpec((tq, LANES), lambda b, h, p, qt, kt: (b * nq + kt[p], h)),
                  pl.BlockSpec((tq, LANES), lambda b, h, p, qt, kt: (b * nq + kt[p], h)),
                  pl.BlockSpec((tq, LANES), lambda b, h, p, qt, kt: (b * nq + qt[p], g_cb + h)),
                  pl.BlockSpec((4, DIFF_DK), lambda b, h, p, qt, kt: (0, 0)),
                  pl.BlockSpec((1, DIFF_DV), lambda b, h, p, qt, kt: (0, 0))],
        out_specs=pl.BlockSpec((tq, LANES), lambda b, h, p, qt, kt: (b * nq + qt[p], h)),
        scratch_shapes=[pltpu.VMEM((2, tq, 1), F32), pltpu.VMEM((2, tq, 1), F32),
                        pltpu.VMEM((2, tq, DIFF_DV), F32)],
    )
    return pl.pallas_call(
        functools.partial(_diff_attn_body, tq=tq, lam_init=lam_init),
        grid_spec=grid_spec,
        out_shape=jax.ShapeDtypeStruct((t, DIFF_WIDTH), BF16),
        compiler_params=_params("parallel", "parallel", "arbitrary"),
        name="diff_attn_prompt",
    )(qi_tab, ki_tab, qb, kb, vb, proj, lam_vecs, sub_g.reshape(1, DIFF_DV))


def _diff_attn_sample_body(pt_ref, qT_ref, kn_ref, vn_ref, g_ref, kp_ref, vp_ref, e2_ref, lam_ref, sg_ref,
                           o_ref, qblk_ref, m_ref, l_ref, acc_ref, *, n_pages, lam_init):
    b = pl.program_id(0)
    p = pl.program_id(1)
    nreq = qT_ref.shape[1]

    @pl.when(p == 0)
    def _():
        onehot = (lax.broadcasted_iota(jnp.int32, (nreq, LANES), 0) == b).astype(BF16)
        qcol = _dot(qT_ref[...], onehot)
        row = lax.broadcasted_iota(jnp.int32, qcol.shape, 0)
        lane = lax.broadcasted_iota(jnp.int32, qcol.shape, 1)
        qblk_ref[...] = jnp.where(row // DIFF_DK == lane, qcol, 0.0).astype(BF16)
        kn = jnp.broadcast_to(kn_ref[...], (SUBLANES, DIFF_QK)).astype(BF16)
        s_self = _dot(kn, qblk_ref[...])[0:1]
        m_ref[...] = s_self
        l_ref[...] = jnp.ones(l_ref.shape, F32)
        sub = lax.broadcasted_iota(jnp.int32, (SUBLANES, DIFF_WIDTH), 0)
        v0 = jnp.where(sub == 0, jnp.broadcast_to(vn_ref[...], (SUBLANES, DIFF_WIDTH)), 0.0)
        acc_ref[0] = v0
        acc_ref[1] = v0

    s = _dot(kp_ref[...].astype(BF16), qblk_ref[...])
    m_old = m_ref[...]
    m_new = jnp.maximum(m_old, jnp.max(s, axis=0, keepdims=True))
    alpha = jnp.exp(m_old - m_new)
    pr = jnp.exp(s - m_new)
    l_ref[...] = alpha * l_ref[...] + jnp.sum(pr, axis=0, keepdims=True)
    m_ref[...] = m_new
    pb = pr.astype(BF16)
    v = vp_ref[...]
    a8 = _split2(jnp.broadcast_to(alpha, (SUBLANES, LANES)))
    for c in range(2):
        e2 = e2_ref[c]
        pexp = _dot(pb, e2)
        contrib = (pexp * v).reshape(PAGE_SIZE // SUBLANES, SUBLANES, DIFF_WIDTH).sum(axis=0)
        acc_ref[c] = acc_ref[c] * _expand(a8, e2) + contrib

    @pl.when(p == n_pages - 1)
    def _():
        l8 = _split3(jnp.broadcast_to(l_ref[...], (SUBLANES, LANES)))
        o = []
        for c in range(2):
            num = jnp.sum(acc_ref[c], axis=0, keepdims=True)
            o.append(num / _expand(l8, e2_ref[c])[0:1])
        lam = _diff_lambda(lam_ref, lam_init)
        od = o[0] - lam * o[1]
        for h in range(DIFF_HEADS):
            sl = slice(h * LANES, (h + 1) * LANES)
            o_ref[:, sl] = _subln_gate(od[:, sl], sg_ref[...], g_ref[:, sl], lam_init).astype(BF16)


def _diff_attn_sample(qT, k_new, proj, cache_k, cache_v, page_table, layer_a, lam_vecs, sub_g, lam_init):
    nreq = k_new.shape[0]
    n_pages = page_table.shape[1]
    n_pool = cache_k.shape[0] // 2
    base = layer_a * n_pool
    col = np.arange(LANES)[:, None]
    lane_head = np.arange(DIFF_WIDTH)[None, :] // DIFF_DV
    e2 = jnp.asarray(np.stack([(col == 2 * lane_head + c) for c in range(2)]).astype(np.float32), BF16)
    r3 = lambda a: a.reshape(nreq, 1, a.shape[-1])
    row = lambda cb: pl.BlockSpec((None, 1, DIFF_QK), lambda b, p, pt: (b, 0, cb))
    page = pl.BlockSpec((None, PAGE_SIZE, DIFF_QK), lambda b, p, pt: (base + pt[b * n_pages + p], 0, 0))
    const = lambda shape: pl.BlockSpec(shape, lambda b, p, pt: (0,) * len(shape))
    grid_spec = pltpu.PrefetchScalarGridSpec(
        num_scalar_prefetch=1,
        grid=(nreq, n_pages),
        in_specs=[const((DIFF_QK, nreq)), row(0), row(3), row(4), page, page,
                  const((2, LANES, DIFF_WIDTH)), const((4, DIFF_DK)), const((1, DIFF_DV))],
        out_specs=row(0),
        scratch_shapes=[pltpu.VMEM((DIFF_QK, LANES), BF16), pltpu.VMEM((1, LANES), F32),
                        pltpu.VMEM((1, LANES), F32), pltpu.VMEM((2, SUBLANES, DIFF_WIDTH), F32)],
    )
    out = pl.pallas_call(
        functools.partial(_diff_attn_sample_body, n_pages=n_pages, lam_init=lam_init),
        grid_spec=grid_spec,
        out_shape=jax.ShapeDtypeStruct((nreq, 1, DIFF_WIDTH), BF16),
        compiler_params=_params("parallel", "arbitrary"),
        name="diff_attn_sample",
    )(page_table.reshape(-1), qT, r3(k_new), r3(proj), r3(proj), cache_k, cache_v, e2, lam_vecs,
      sub_g.reshape(1, DIFF_DV))
    return out.reshape(nreq, DIFF_WIDTH)


def _xattn_heads(xq, xg, mk, mv):
    outs = []
    for h in range(X_HEADS):
        sl = slice(h * X_HEADDIM, (h + 1) * X_HEADDIM)
        s = _dot_nt(xq[:, sl].astype(BF16), mk[:, sl].astype(BF16)) * (X_HEADDIM ** -0.5)
        e = jnp.exp(s - jnp.max(s, axis=-1, keepdims=True))
        pr = (e / jnp.sum(e, axis=-1, keepdims=True)).astype(BF16)
        outs.append(_dot(pr, mv[:, sl].astype(BF16)) * _silu(xg[:, sl]))
    return outs


def _xattn_prompt_body(xq_ref, xg_ref, mk_ref, mv_ref, o_ref):
    outs = _xattn_heads(xq_ref[...], xg_ref[...], mk_ref[...], mv_ref[...])
    for h in range(X_HEADS):
        o_ref[:, h * X_HEADDIM:(h + 1) * X_HEADDIM] = outs[h].astype(BF16)


def _xattn_prompt(proj, xq_cb, mem_kv, layer, nb, s_len, tq):
    t = proj.shape[0]
    nq = s_len // tq
    return pl.pallas_call(
        _xattn_prompt_body,
        grid=(nb, nq),
        in_specs=[pl.BlockSpec((tq, X_WIDTH), lambda b, i: (b * nq + i, xq_cb)),
                  pl.BlockSpec((tq, X_WIDTH), lambda b, i: (b * nq + i, xq_cb + 1)),
                  pl.BlockSpec((None, N_MEM, X_WIDTH), lambda b, i: (b, 0, 2 * layer)),
                  pl.BlockSpec((None, N_MEM, X_WIDTH), lambda b, i: (b, 0, 2 * layer + 1))],
        out_specs=pl.BlockSpec((tq, X_WIDTH), lambda b, i: (b * nq + i, 0)),
        out_shape=jax.ShapeDtypeStruct((t, X_WIDTH), BF16),
        compiler_params=_params("parallel", "parallel"),
        name="xattn_prompt",
    )(proj, proj, mem_kv, mem_kv)


def _xattn_sample_body(xq_ref, xg_ref, mk_ref, mv_ref, o_ref, *, rb):
    for r in range(rb):
        xq = jnp.broadcast_to(xq_ref[r], (SUBLANES, X_WIDTH))
        xg = jnp.broadcast_to(xg_ref[r], (SUBLANES, X_WIDTH))
        outs = _xattn_heads(xq, xg, mk_ref[r], mv_ref[r])
        for h in range(X_HEADS):
            o_ref[r, :, h * X_HEADDIM:(h + 1) * X_HEADDIM] = outs[h][0:1].astype(BF16)


def _xattn_sample(proj, xq_cb, mem_k, mem_v, rb):
    nreq = proj.shape[0]
    p3 = proj.reshape(nreq, 1, proj.shape[1])
    out = pl.pallas_call(
        functools.partial(_xattn_sample_body, rb=rb),
        grid=(nreq // rb,),
        in_specs=[pl.BlockSpec((rb, 1, X_WIDTH), lambda i: (i, 0, xq_cb)),
                  pl.BlockSpec((rb, 1, X_WIDTH), lambda i: (i, 0, xq_cb + 1)),
                  pl.BlockSpec((rb, N_MEM, X_WIDTH), lambda i: (i, 0, 0)),
                  pl.BlockSpec((rb, N_MEM, X_WIDTH), lambda i: (i, 0, 0))],
        out_specs=pl.BlockSpec((rb, 1, X_WIDTH), lambda i: (i, 0, 0)),
        out_shape=jax.ShapeDtypeStruct((nreq, 1, X_WIDTH), BF16),
        compiler_params=_params("parallel"),
        name="xattn_sample",
    )(p3, p3, mem_k, mem_v)
    return out.reshape(nreq, X_WIDTH)


def _ssd_consts(dt_bias, a_log, d_skip):
    pad = LANES - SSD_HEADS
    dtb = jnp.pad(dt_bias.astype(F32), (0, pad)).reshape(1, LANES)
    alog = jnp.pad(a_log.astype(F32), (0, pad)).reshape(1, LANES)
    dexp = jnp.repeat(d_skip.astype(F32), SSD_HEADDIM).reshape(1, SSD_INNER)
    head = np.arange(LANES)[:, None]
    chan_head = np.arange(SSD_INNER)[None, :] // SSD_HEADDIM
    eexp = jnp.asarray((head == chan_head).astype(np.float32), BF16)
    return dtb, alog, dexp, eexp


def _softplus(x):
    return jnp.maximum(x, 0.0) + jnp.log1p(jnp.exp(-jnp.abs(x)))


def _ssd_gate_norm(y, xs, z, dexp, ng):
    y = (y + dexp * xs) * _silu(z)
    r = lax.rsqrt(jnp.mean(y * y, axis=-1, keepdims=True) + EPS)
    return ((y * r) * ng).astype(BF16)


def _ssd_prompt_body(z_ref, xbc_ref, dtr_ref, cw_ref, cb_ref, dtb_ref, alog_ref, dexp_ref, ng_ref, eexp_ref,
                     y_ref, hout_ref, ext_ref, h_ref, yd_ref):
    q = SSD_CHUNK
    c = pl.program_id(1)

    @pl.when(c == 0)
    def _():
        ext_ref[0:SUBLANES, :] = jnp.zeros((SUBLANES, SSD_CONV_CH), F32)
        h_ref[...] = jnp.zeros(h_ref.shape, F32)

    x_raw = xbc_ref[...]
    ext_ref[SUBLANES:SUBLANES + q, :] = x_raw
    conv = cb_ref[...] + cw_ref[SSD_CONV - 1:SSD_CONV, :] * x_raw
    for k in range(1, SSD_CONV):
        conv = conv + cw_ref[SSD_CONV - 1 - k:SSD_CONV - k, :] * ext_ref[SUBLANES - k:SUBLANES - k + q, :]
    ext_ref[0:SUBLANES, :] = x_raw[q - SUBLANES:q, :]
    xbc = _silu(conv)
    xs = xbc[:, :SSD_INNER]
    bm = xbc[:, SSD_INNER:SSD_INNER + SSD_GROUPS * SSD_STATE].astype(BF16)
    cm = xbc[:, SSD_INNER + SSD_GROUPS * SSD_STATE:].astype(BF16)

    eexp = eexp_ref[...]
    dt = _softplus(dtr_ref[...] + dtb_ref[...])
    da = dt * (-jnp.exp(alog_ref[...]))
    ri = lax.broadcasted_iota(jnp.int32, (q, q), 0)
    ci = lax.broadcasted_iota(jnp.int32, (q, q), 1)
    tri = ri >= ci
    tri_b = tri.astype(BF16)
    hi, mid, lo = _split3(da)
    acum = _dot(tri_b, hi) + _dot(tri_b, mid) + _dot(tri_b, lo)
    acum_t = acum.T
    a_last = acum[q - 1:q, :]
    xdt = xs * _expand(_split2(dt), eexp)
    xdt_b = xdt.astype(BF16)
    ea_exp = _expand(_split2(jnp.exp(acum)), eexp)
    wend_exp = _expand(_split2(jnp.exp(a_last - acum)), eexp)
    xw_t = (xdt * wend_exp).T.astype(BF16)
    decay = jnp.exp(a_last)

    lane = lax.broadcasted_iota(jnp.int32, (q, LANES), 1)
    for g in range(SSD_GROUPS):
        gs = slice(g * SSD_STATE, (g + 1) * SSD_STATE)
        ch = slice(g * SSD_HPG * SSD_HEADDIM, (g + 1) * SSD_HPG * SSD_HEADDIM)
        scores = _dot_nt(cm[:, gs], bm[:, gs])
        for pair in range(SSD_HPG // 2):
            halves = []
            for r in range(2):
                hd = g * SSD_HPG + 2 * pair + r
                seg = acum[:, hd:hd + 1] - acum_t[hd:hd + 1, :]
                ldec = jnp.where(tri, jnp.exp(jnp.where(tri, seg, 0.0)), 0.0)
                xp = xdt_b[:, (hd // 2) * LANES:(hd // 2 + 1) * LANES]
                halves.append(_dot((scores * ldec).astype(BF16), xp))
            lo_lane = (g * SSD_HPG + 2 * pair) // 2 * LANES
            yd_ref[:, lo_lane:lo_lane + LANES] = jnp.where(lane < SSD_HEADDIM, halves[0], halves[1])
        h_g = h_ref[ch, :]
        yd_ref[:, ch] = yd_ref[:, ch] + _dot_nt(cm[:, gs], h_g.astype(BF16)) * ea_exp[:, ch]
        upd = _dot(xw_t[ch, :], bm[:, gs])
        for r in range(SSD_HPG):
            hd = g * SSD_HPG + r
            rows = slice(hd * SSD_HEADDIM, (hd + 1) * SSD_HEADDIM)
            h_ref[rows, :] = h_ref[rows, :] * decay[:, hd:hd + 1] + upd[r * SSD_HEADDIM:(r + 1) * SSD_HEADDIM, :]

    y_ref[...] = _ssd_gate_norm(yd_ref[...], xs, z_ref[...], dexp_ref[...], ng_ref[...])
    hout_ref[...] = h_ref[...]


def _ssd_prompt(proj, proj_s, conv_w, conv_b, consts, norm_g, nb, s_len):
    t = proj.shape[0]
    q = SSD_CHUNK
    nc = s_len // q
    dtb, alog, dexp, eexp = consts
    const = lambda shape: pl.BlockSpec(shape, lambda b, c: (0,) * len(shape))
    return pl.pallas_call(
        _ssd_prompt_body,
        grid=(nb, nc),
        in_specs=[pl.BlockSpec((q, SSD_INNER), lambda b, c: (b * nc + c, 0)),
                  pl.BlockSpec((q, SSD_CONV_CH), lambda b, c: (b * nc + c, 0)),
                  pl.BlockSpec((q, LANES), lambda b, c: (b * nc + c, SSD_CONV_CH // LANES)),
                  const((SSD_CONV, SSD_CONV_CH)), const((1, SSD_CONV_CH)), const((1, LANES)), const((1, LANES)),
                  const((1, SSD_INNER)), const((1, SSD_INNER)), const((LANES, SSD_INNER))],
        out_specs=[pl.BlockSpec((q, SSD_INNER), lambda b, c: (b * nc + c, 0)),
                   pl.BlockSpec((None, SSD_INNER, SSD_STATE), lambda b, c: (b, 0, 0))],
        out_shape=[jax.ShapeDtypeStruct((t, SSD_INNER), BF16),
                   jax.ShapeDtypeStruct((nb, SSD_INNER, SSD_STATE), F32)],
        scratch_shapes=[pltpu.VMEM((SUBLANES + q, SSD_CONV_CH), F32), pltpu.VMEM((SSD_INNER, SSD_STATE), F32),
                        pltpu.VMEM((q, SSD_INNER), F32)],
        compiler_params=_params("parallel", "arbitrary"),
        name="ssd_prompt",
    )(proj, proj_s, proj_s, conv_w, conv_b.reshape(1, -1), dtb, alog, dexp, norm_g.reshape(1, -1), eexp)


def _ssd_sample_body(z_ref, xbc_ref, dtr_ref, cs_ref, hin_ref, cw_ref, cb_ref, dtb_ref, alog_ref, dexp_ref, ng_ref,
                     eexp_ref, y_ref, hout_ref, xs_ref, xdt_t_ref, da_hi_ref, da_lo_ref, bm_ref, cm_t_ref, y_t_ref,
                     *, rb):
    i = pl.program_id(0)
    nreq = z_ref.shape[0]

    @pl.when(i == 0)
    def _():
        conv = cb_ref[...] + cw_ref[SSD_CONV - 1:SSD_CONV, :] * xbc_ref[...]
        for k in range(SSD_CONV - 1):
            conv = conv + cw_ref[k:k + 1, :] * cs_ref[k]
        xbc = _silu(conv)
        xs = xbc[:, :SSD_INNER]
        xs_ref[...] = xs
        bm_ref[...] = xbc[:, SSD_INNER:SSD_INNER + SSD_GROUPS * SSD_STATE]
        cm_t_ref[...] = xbc[:, SSD_INNER + SSD_GROUPS * SSD_STATE:].T
        dt = _softplus(dtr_ref[...] + dtb_ref[...])
        da_hi, da_lo = _split2(jnp.exp(dt * (-jnp.exp(alog_ref[...]))).T)
        da_hi_ref[...] = da_hi
        da_lo_ref[...] = da_lo
        xdt_t_ref[...] = (xs * _expand(_split2(dt), eexp_ref[...])).T.astype(BF16)
        y_t_ref[...] = jnp.zeros(y_t_ref.shape, F32)

    def per_request(r, carry):
        b = i * rb + r
        onehot = (lax.broadcasted_iota(jnp.int32, (nreq, LANES), 0) == b).astype(BF16)
        xb = _dot(xdt_t_ref[...], onehot)
        dab = _dot(da_hi_ref[...], onehot) + _dot(da_lo_ref[...], onehot)
        brow = bm_ref[pl.ds(b, 1), :]
        cmask = lax.broadcasted_iota(jnp.int32, (SSD_STATE, nreq), 1) == b
        for g in range(SSD_GROUPS):
            gs = slice(g * SSD_STATE, (g + 1) * SSD_STATE)
            ch = slice(g * SSD_HPG * SSD_HEADDIM, (g + 1) * SSD_HPG * SSD_HEADDIM)
            for hr in range(SSD_HPG):
                hd = g * SSD_HPG + hr
                rows = slice(hd * SSD_HEADDIM, (hd + 1) * SSD_HEADDIM)
                hout_ref[r, rows, :] = dab[hd:hd + 1, :] * hin_ref[r, rows, :] + xb[rows, :] * brow[:, gs]
            cmat = jnp.where(cmask, cm_t_ref[gs, :], 0.0).astype(BF16)
            y_t_ref[ch, :] = y_t_ref[ch, :] + _dot(hout_ref[r, ch, :].astype(BF16), cmat)
        return carry

    lax.fori_loop(0, rb, per_request, 0)

    @pl.when(i == pl.num_programs(0) - 1)
    def _():
        y_ref[...] = _ssd_gate_norm(y_t_ref[...].T, xs_ref[...], z_ref[...], dexp_ref[...], ng_ref[...])


def _ssd_sample(proj, proj_s, conv_state_t, h_state, conv_w, conv_b, consts, norm_g, rb):
    nreq = proj.shape[0]
    dtb, alog, dexp, eexp = consts
    const = lambda shape: pl.BlockSpec(shape, lambda i: (0,) * len(shape))
    return pl.pallas_call(
        functools.partial(_ssd_sample_body, rb=rb),
        grid=(nreq // rb,),
        in_specs=[pl.BlockSpec((nreq, SSD_INNER), lambda i: (0, 0)),
                  pl.BlockSpec((nreq, SSD_CONV_CH), lambda i: (0, 0)),
                  pl.BlockSpec((nreq, LANES), lambda i: (0, SSD_CONV_CH // LANES)),
                  const((SSD_CONV - 1, nreq, SSD_CONV_CH)),
                  pl.BlockSpec((rb, SSD_INNER, SSD_STATE), lambda i: (i, 0, 0)),
                  const((SSD_CONV, SSD_CONV_CH)), const((1, SSD_CONV_CH)), const((1, LANES)), const((1, LANES)),
                  const((1, SSD_INNER)), const((1, SSD_INNER)), const((LANES, SSD_INNER))],
        out_specs=[const((nreq, SSD_INNER)),
                   pl.BlockSpec((rb, SSD_INNER, SSD_STATE), lambda i: (i, 0, 0))],
        out_shape=[jax.ShapeDtypeStruct((nreq, SSD_INNER), BF16),
                   jax.ShapeDtypeStruct(h_state.shape, F32)],
        scratch_shapes=[pltpu.VMEM((nreq, SSD_INNER), F32), pltpu.VMEM((SSD_INNER, nreq), BF16),
                        pltpu.VMEM((LANES, nreq), BF16), pltpu.VMEM((LANES, nreq), BF16),
                        pltpu.VMEM((nreq, SSD_GROUPS * SSD_STATE), F32),
                        pltpu.VMEM((SSD_GROUPS * SSD_STATE, nreq), F32), pltpu.VMEM((SSD_INNER, nreq), F32)],
        compiler_params=_params("arbitrary"),
        name="ssd_sample",
    )(proj, proj_s, proj_s, conv_state_t, h_state, conv_w, conv_b.reshape(1, -1), dtb, alog, dexp,
      norm_g.reshape(1, -1), eexp)


CONF_PAD = 32
CONF_RB = 64
CONF_CB = LANES
CONF_NCB = CONF_WIDTH // CONF_CB


def _conf_finish(c, g, ln_g, ln_b):
    mu = jnp.mean(c, axis=-1, keepdims=True)
    xc = c - mu
    r = lax.rsqrt(jnp.mean(xc * xc, axis=-1, keepdims=True) + EPS)
    return (_silu((xc * r) * ln_g + ln_b) * _silu(g)).astype(BF16)


def _conf_prompt_body(a_ref, ag_ref, g_ref, w_ref, b_ref, lg_ref, lb_ref, y_ref, tail_ref, sh_ref, c_ref, *, tt):
    i = pl.program_id(1)
    rows = CONF_PAD + tt

    @pl.when(i == 0)
    def _():
        sh_ref[0, :, 0:CONF_PAD, :] = jnp.zeros((CONF_NCB, CONF_PAD, CONF_CB), F32)

    a = a_ref[...]
    ag = ag_ref[...]
    u = a / (1.0 + jnp.exp(-ag))
    tail_ref[...] = u[tt - CONF_PAD:tt, :]
    for cb in range(CONF_NCB):
        sh_ref[0, cb, CONF_PAD:rows, :] = u[:, cb * CONF_CB:(cb + 1) * CONF_CB]
        for s in range(1, SUBLANES):
            sh_ref[s, cb, 0:rows - SUBLANES, :] = sh_ref[0, cb, s:s + rows - SUBLANES, :]

    first = CONF_PAD - (CONF_CONV_W - 1)

    def tile(idx, carry):
        rb = idx // CONF_NCB
        cb = idx % CONF_NCB
        r0 = pl.multiple_of(rb * CONF_RB, CONF_RB)
        acc = jnp.broadcast_to(b_ref[cb], (CONF_RB, CONF_CB))
        for k in range(CONF_CONV_W):
            off = first + k
            src = sh_ref[off % SUBLANES, cb, pl.ds(r0 + (off // SUBLANES) * SUBLANES, CONF_RB), :]
            acc = acc + w_ref[k, cb] * src
        c_ref[cb, pl.ds(r0, CONF_RB), :] = acc
        return carry

    lax.fori_loop(0, (tt // CONF_RB) * CONF_NCB, tile, 0)
    for cb in range(CONF_NCB):
        sh_ref[0, cb, 0:CONF_PAD, :] = sh_ref[0, cb, tt:rows, :]
    c = jnp.concatenate([c_ref[cb] for cb in range(CONF_NCB)], axis=1)
    y_ref[...] = _conf_finish(c, g_ref[...], lg_ref[...], lb_ref[...])


def _conf_prompt(proj, dw_w, dw_b, ln_g, ln_b, nb, s_len, tt):
    t = proj.shape[0]
    nt = s_len // tt
    col = lambda cb: pl.BlockSpec((tt, CONF_WIDTH), lambda b, i: (b * nt + i, cb))
    const = lambda shape: pl.BlockSpec(shape, lambda b, i: (0,) * len(shape))
    return pl.pallas_call(
        functools.partial(_conf_prompt_body, tt=tt),
        grid=(nb, nt),
        in_specs=[col(0), col(1), col(2), const((CONF_CONV_W, CONF_NCB, 1, CONF_CB)), const((CONF_NCB, 1, CONF_CB)),
                  const((1, CONF_WIDTH)), const((1, CONF_WIDTH))],
        out_specs=[pl.BlockSpec((tt, CONF_WIDTH), lambda b, i: (b * nt + i, 0)),
                   pl.BlockSpec((None, CONF_PAD, CONF_WIDTH), lambda b, i: (b, 0, 0))],
        out_shape=[jax.ShapeDtypeStruct((t, CONF_WIDTH), BF16),
                   jax.ShapeDtypeStruct((nb, CONF_PAD, CONF_WIDTH), F32)],
        scratch_shapes=[pltpu.VMEM((SUBLANES, CONF_NCB, CONF_PAD + tt, CONF_CB), F32),
                        pltpu.VMEM((CONF_NCB, tt, CONF_CB), F32)],
        compiler_params=_params("parallel", "arbitrary"),
        name="conf_prompt",
    )(proj, proj, proj, dw_w.reshape(CONF_CONV_W, CONF_NCB, 1, CONF_CB), dw_b.reshape(CONF_NCB, 1, CONF_CB),
      ln_g.reshape(1, -1), ln_b.reshape(1, -1))


def _conf_sample_body(a_ref, ag_ref, g_ref, st_ref, w_ref, b_ref, lg_ref, lb_ref, y_ref, u_ref, c_ref, *, tc):
    j = pl.program_id(0)
    u = a_ref[...] / (1.0 + jnp.exp(-ag_ref[...]))
    u_ref[...] = u
    acc = b_ref[...] + w_ref[CONF_CONV_W - 1:CONF_CONV_W, :] * u
    for k in range(CONF_CONV_W - 1):
        acc = acc + w_ref[k:k + 1, :] * st_ref[k]
    c_ref[j] = acc

    @pl.when(j == pl.num_programs(0) - 1)
    def _():
        c = jnp.concatenate([c_ref[k] for k in range(CONF_WIDTH // tc)], axis=1)
        y_ref[...] = _conf_finish(c, g_ref[...], lg_ref[...], lb_ref[...])


def _conf_sample(proj, state_t, dw_w, dw_b, ln_g, ln_b, tc):
    nreq = proj.shape[0]
    nblk = CONF_WIDTH // tc
    col = lambda off: pl.BlockSpec((nreq, tc), lambda j: (0, off * nblk + j))
    vec = pl.BlockSpec((1, tc), lambda j: (0, j))
    full = pl.BlockSpec((1, CONF_WIDTH), lambda j: (0, 0))
    return pl.pallas_call(
        functools.partial(_conf_sample_body, tc=tc),
        grid=(nblk,),
        in_specs=[col(0), col(1), pl.BlockSpec((nreq, CONF_WIDTH), lambda j: (0, 2)),
                  pl.BlockSpec((CONF_CONV_W - 1, nreq, tc), lambda j: (0, 0, j)),
                  pl.BlockSpec((CONF_CONV_W, tc), lambda j: (0, j)), vec, full, full],
        out_specs=[pl.BlockSpec((nreq, CONF_WIDTH), lambda j: (0, 0)),
                   pl.BlockSpec((nreq, tc), lambda j: (0, j))],
        out_shape=[jax.ShapeDtypeStruct((nreq, CONF_WIDTH), BF16),
                   jax.ShapeDtypeStruct((nreq, CONF_WIDTH), F32)],
        scratch_shapes=[pltpu.VMEM((nblk, nreq, tc), F32)],
        compiler_params=_params("arbitrary"),
        name="conf_sample",
    )(proj, proj, proj, state_t, dw_w, dw_b.reshape(1, -1), ln_g.reshape(1, -1), ln_b.reshape(1, -1))


def _even_weights(w_in):
    cuts = np.cumsum([SSD_INNER, SSD_CONV_CH, SSD_HEADS]).tolist()
    z, xbc, dt, rest = (w_in[:, :cuts[0]], w_in[:, cuts[0]:cuts[1]], w_in[:, cuts[1]:cuts[2]], w_in[:, cuts[2]:])
    main = jnp.concatenate([z, rest], axis=1).astype(BF16)
    pad = jnp.zeros((w_in.shape[0], LANES - SSD_HEADS), w_in.dtype)
    ssd = jnp.concatenate([xbc, dt, pad], axis=1).astype(BF16)
    return main, ssd


def kernel(x_prompt, x_sample, cache_attn_k, cache_attn_v, cache_mem_k, cache_mem_v, state_ssm, state_ssm_conv,
           state_conf_conv, page_table, mem_prompt, norm_a, w_in_a, ssm_conv_w, ssm_conv_b, ssm_dt_bias, ssm_a_log,
           ssm_d, ssm_norm, diff_lambda, diff_subln, w_out_a, norm_c, w_in_c, conf_dw_w, conf_dw_b, conf_ln_g,
           conf_ln_b, w_out_c, xattn_wk, xattn_wv, mem_norm, norm_f):
    bp, sp, d = x_prompt.shape
    bs = x_sample.shape[0]
    n_a = w_in_a.shape[0]
    n_pool = cache_attn_k.shape[1]
    past = page_table.shape[1] * PAGE_SIZE
    tp = bp * sp

    tm_p = min(1024, sp)
    tq_attn = min(512, sp)
    tq_x = min(512, sp)
    tt_conf = min(256, sp)
    tm_out = min(512, sp)

    xp = x_prompt.reshape(tp, d)
    xs = x_sample.reshape(bs, d)

    w_mem = jnp.concatenate([jnp.concatenate([xattn_wk[l], xattn_wv[l]], axis=1) for l in range(DEPTH)],
                            axis=1).astype(BF16)
    mem_kv = _norm_matmul(mem_prompt.reshape(bp * N_MEM, d), mem_norm, w_mem, N_MEM, 1024)
    mem_kv3 = mem_kv.reshape(bp, N_MEM, DEPTH * 2 * X_WIDTH)
    mem_kv5 = mem_kv.reshape(bp, N_MEM, DEPTH, 2, X_HEADS, X_HEADDIM)
    p_mem_k = jnp.moveaxis(mem_kv5[:, :, :, 0], 2, 0)
    p_mem_v = jnp.moveaxis(mem_kv5[:, :, :, 1], 2, 0)

    tab_p = _rope_tables(jnp.arange(sp))
    tab_s = _rope_tables(jnp.full((bs,), past, jnp.int32))
    cache_k2 = cache_attn_k.reshape(n_a * n_pool, PAGE_SIZE, DIFF_QK)
    cache_v2 = cache_attn_v.reshape(n_a * n_pool, PAGE_SIZE, DIFF_WIDTH)

    pk, pv, pssm, pconv, pconf = [], [], [], [], []
    sk, sv, sssm, sconv, sconf = [], [], [], [], []
    for layer in range(DEPTH):
        j = layer // 2
        final = layer == DEPTH - 1
        mem_k_s = cache_mem_k[layer].reshape(bs, N_MEM, X_WIDTH)
        mem_v_s = cache_mem_v[layer].reshape(bs, N_MEM, X_WIDTH)
        if layer % 2 == 0:
            lam_init = 0.8 - 0.6 * math.exp(-0.3 * layer)
            w_main, w_ssd = _even_weights(w_in_a[j])
            w_out = w_out_a[j].astype(BF16)
            consts = _ssd_consts(ssm_dt_bias[j], ssm_a_log[j], ssm_d[j])
            xq_cb = (SSD_INNER + 2 * DIFF_QK + 2 * DIFF_WIDTH) // X_WIDTH
            v_lo = SSD_INNER + 2 * DIFF_QK

            proj = _norm_matmul(xp, norm_a[j], w_main, tm_p, 1024)
            proj_s = _norm_matmul(xp, norm_a[j], w_ssd, tm_p, w_ssd.shape[1])
            y_ssd, h_last = _ssd_prompt(proj, proj_s, ssm_conv_w[j], ssm_conv_b[j], consts, ssm_norm[j], bp, sp)
            qb, k_rot, kb, vb = _rope(proj, tab_p, sp, tm_p)
            y_diff = _diff_attn_prompt(qb, kb, vb, proj, diff_lambda[j], diff_subln[j], lam_init, bp, sp, tq_attn)
            y_x = _xattn_prompt(proj, xq_cb, mem_kv3, layer, bp, sp, tq_x)
            xp = _out_proj([y_ssd, y_diff, y_x], w_out, xp, norm_f, final, tm_out)
            pk.append(k_rot.reshape(bp, sp, DIFF_HEADS, 2, DIFF_DK))
            pv.append(proj[:, v_lo:v_lo + DIFF_WIDTH].reshape(bp, sp, DIFF_HEADS, DIFF_DV))
            pssm.append(h_last.reshape(bp, SSD_GROUPS, SSD_HPG, SSD_HEADDIM, SSD_STATE))
            pconv.append(proj_s.reshape(bp, sp, -1)[:, sp - (SSD_CONV - 1):, :SSD_CONV_CH])

            proj = _norm_matmul(xs, norm_a[j], w_main, bs, 1024)
            proj_s = _norm_matmul(xs, norm_a[j], w_ssd, bs, w_ssd.shape[1])
            conv_t = jnp.swapaxes(state_ssm_conv[j], 0, 1)
            h_in = state_ssm[j].reshape(bs, SSD_INNER, SSD_STATE)
            y_ssd, h_new = _ssd_sample(proj, proj_s, conv_t, h_in, ssm_conv_w[j], ssm_conv_b[j], consts,
                                       ssm_norm[j], 8)
            q_t, k_rot, _, _ = _rope(proj, tab_s, bs, bs, transpose_q=True)
            y_diff = _diff_attn_sample(q_t, k_rot, proj, cache_k2, cache_v2, page_table, j, diff_lambda[j],
                                       diff_subln[j], lam_init)
            y_x = _xattn_sample(proj, xq_cb, mem_k_s, mem_v_s, 4)
            xs = _out_proj([y_ssd, y_diff, y_x], w_out, xs, norm_f, final, bs)
            sk.append(k_rot.reshape(bs, 1, DIFF_HEADS, 2, DIFF_DK))
            sv.append(proj[:, v_lo:v_lo + DIFF_WIDTH].reshape(bs, 1, DIFF_HEADS, DIFF_DV))
            sssm.append(h_new.reshape(bs, SSD_GROUPS, SSD_HPG, SSD_HEADDIM, SSD_STATE))
            sconv.append(jnp.concatenate([state_ssm_conv[j][:, 1:], proj_s[:, None, :SSD_CONV_CH]], axis=1))
        else:
            w_in = w_in_c[j].astype(BF16)
            w_out = w_out_c[j].astype(BF16)
            xq_cb = 3 * CONF_WIDTH // X_WIDTH

            proj = _norm_matmul(xp, norm_c[j], w_in, tm_p, 1024)
            y_conf, tail = _conf_prompt(proj, conf_dw_w[j], conf_dw_b[j], conf_ln_g[j], conf_ln_b[j], bp, sp, tt_conf)
            y_x = _xattn_prompt(proj, xq_cb, mem_kv3, layer, bp, sp, tq_x)
            xp = _out_proj([y_conf, y_x], w_out, xp, norm_f, final, tm_out)
            pconf.append(tail[:, CONF_PAD - (CONF_CONV_W - 1):])

            proj = _norm_matmul(xs, norm_c[j], w_in, bs, 1024)
            st_t = jnp.swapaxes(state_conf_conv[j], 0, 1)
            y_conf, u = _conf_sample(proj, st_t, conf_dw_w[j], conf_dw_b[j], conf_ln_g[j], conf_ln_b[j], 256)
            y_x = _xattn_sample(proj, xq_cb, mem_k_s, mem_v_s, 4)
            xs = _out_proj([y_conf, y_x], w_out, xs, norm_f, final, bs)
            sconf.append(jnp.concatenate([state_conf_conv[j][:, 1:], u[:, None]], axis=1))

    return (xp.reshape(bp, sp, d), xs.reshape(bs, 1, d), jnp.stack(pk), jnp.stack(pv), jnp.stack(pssm),
            jnp.stack(pconv), jnp.stack(pconf), p_mem_k, p_mem_v, jnp.stack(sk), jnp.stack(sv), jnp.stack(sssm),
            jnp.stack(sconv), jnp.stack(sconf))
```

```python
import functools
import math

import jax
import jax.numpy as jnp
import numpy as np
from jax import lax
from jax.experimental import pallas as pl
from jax.experimental.pallas import tpu as pltpu

F32 = jnp.float32
BF16 = jnp.bfloat16

D_MODEL = 1024
DEPTH = 4
PAGE_SIZE = 128
SSD_INNER = 1024
SSD_HEADDIM = 64
SSD_HEADS = 16
SSD_GROUPS = 2
SSD_HPG = 8
SSD_STATE = 128
SSD_CONV = 4
SSD_CHUNK = 128
SSD_CONV_CH = 1536
DIFF_HEADS = 8
DIFF_DK = 64
DIFF_DV = 128
DIFF_QK = 1024
DIFF_WIDTH = 1024
ROT_DIM = 16
ROPE_THETA = 500000.0
CONF_WIDTH = 2048
CONF_CONV_W = 31
N_MEM = 256
X_HEADS = 4
X_HEADDIM = 128
X_WIDTH = 512
EPS = 1e-6

LANES = 128
SUBLANES = 8
VMEM_LIMIT_BYTES = 56 * 1024 * 1024
NEG_BIG = -1e30


def _params(*sem):
    return pltpu.CompilerParams(dimension_semantics=sem, vmem_limit_bytes=VMEM_LIMIT_BYTES)


def _silu(x):
    return x / (1.0 + jnp.exp(-x))


def _split2(x):
    hi = x.astype(BF16)
    lo = (x - hi.astype(F32)).astype(BF16)
    return hi, lo


def _split3(x):
    hi = x.astype(BF16)
    r = x - hi.astype(F32)
    mid = r.astype(BF16)
    lo = (r - mid.astype(F32)).astype(BF16)
    return hi, mid, lo


def _dot(a, b):
    return jnp.dot(a, b, preferred_element_type=F32)


def _dot_nt(a, b):
    return lax.dot_general(a, b, (((1,), (1,)), ((), ())), preferred_element_type=F32)


def _expand(parts, e):
    acc = _dot(parts[0], e)
    for p in parts[1:]:
        acc = acc + _dot(p, e)
    return acc


def _norm_matmul_body(x_ref, g_ref, w_ref, o_ref, h_ref):
    @pl.when(pl.program_id(1) == 0)
    def _():
        x = x_ref[...]
        r = lax.rsqrt(jnp.mean(x * x, axis=-1, keepdims=True) + EPS)
        h_ref[...] = ((x * r) * g_ref[...]).astype(BF16)

    o_ref[...] = _dot(h_ref[...], w_ref[...])


def _norm_matmul(x, g, w, tm, tn):
    m, d = x.shape
    n = w.shape[1]
    return pl.pallas_call(
        _norm_matmul_body,
        grid=(m // tm, n // tn),
        in_specs=[pl.BlockSpec((tm, d), lambda i, j: (i, 0)),
                  pl.BlockSpec((1, d), lambda i, j: (0, 0)),
                  pl.BlockSpec((d, tn), lambda i, j: (0, j))],
        out_specs=pl.BlockSpec((tm, tn), lambda i, j: (i, j)),
        out_shape=jax.ShapeDtypeStruct((m, n), F32),
        scratch_shapes=[pltpu.VMEM((tm, d), BF16)],
        compiler_params=_params("parallel", "arbitrary"),
        name="norm_matmul",
    )(x, g.reshape(1, d), w)


def _out_proj_body(*refs, n_in, final):
    ins = refs[:n_in]
    ws = refs[n_in:2 * n_in]
    x_ref = refs[2 * n_in]
    g_ref = refs[2 * n_in + 1]
    o_ref = refs[2 * n_in + 2]
    acc = x_ref[...]
    for a, w in zip(ins, ws):
        acc = acc + _dot(a[...], w[...])
    if final:
        r = lax.rsqrt(jnp.mean(acc * acc, axis=-1, keepdims=True) + EPS)
        acc = (acc * r) * g_ref[...]
    o_ref[...] = acc


def _out_proj(parts, w, x, norm_f, final, tm):
    m, d = x.shape
    widths = [p.shape[1] for p in parts]
    offs = np.cumsum([0] + widths).tolist()
    ws = [w[offs[i]:offs[i + 1]] for i in range(len(parts))]
    in_specs = ([pl.BlockSpec((tm, wd), lambda i: (i, 0)) for wd in widths]
                + [pl.BlockSpec((wd, d), lambda i: (0, 0)) for wd in widths]
                + [pl.BlockSpec((tm, d), lambda i: (i, 0)), pl.BlockSpec((1, d), lambda i: (0, 0))])
    return pl.pallas_call(
        functools.partial(_out_proj_body, n_in=len(parts), final=final),
        grid=(m // tm,),
        in_specs=in_specs,
        out_specs=pl.BlockSpec((tm, d), lambda i: (i, 0)),
        out_shape=jax.ShapeDtypeStruct((m, d), F32),
        compiler_params=_params("parallel"),
        name="out_proj",
    )(*parts, *ws, x, norm_f.reshape(1, d))


def _rope_tables(pos):
    half = ROT_DIM // 2
    inv_freq = ROPE_THETA ** (-jnp.arange(0, ROT_DIM, 2, dtype=F32) / ROT_DIM)
    ang = pos.astype(F32)[:, None] * inv_freq[None, :]
    cos, sin = jnp.cos(ang), jnp.sin(ang)
    n = pos.shape[0]
    zeros = jnp.zeros((n, DIFF_DK - ROT_DIM), F32)
    z8 = jnp.zeros((n, half), F32)
    c = jnp.concatenate([cos, cos, zeros + 1.0], axis=-1)
    sa = jnp.concatenate([-sin, z8, zeros], axis=-1)
    sb = jnp.concatenate([z8, sin, zeros], axis=-1)
    return (jnp.concatenate([c, c], axis=-1), jnp.concatenate([sa, sa], axis=-1),
            jnp.concatenate([sb, sb], axis=-1))


def _rope_body(q_ref, k_ref, v_ref, c_ref, sa_ref, sb_ref, qt_ref, kb_ref, kt_ref, vt_ref):
    c, sa, sb = c_ref[...], sa_ref[...], sb_ref[...]
    half = ROT_DIM // 2

    def rot(x):
        return x * c + pltpu.roll(x, LANES - half, 1) * sa + pltpu.roll(x, half, 1) * sb

    for h in range(DIFF_HEADS):
        sl = slice(h * LANES, (h + 1) * LANES)
        kr = rot(k_ref[:, sl])
        kb_ref[:, sl] = kr.astype(BF16)
        kt_ref[sl, :] = kr.T
        qt_ref[sl, :] = (rot(q_ref[:, sl]) * (DIFF_DK ** -0.5)).T.astype(BF16)
        vt_ref[sl, :] = v_ref[:, sl].T.astype(BF16)


def _rope(proj, tables, nb, s_len, tm):
    t = proj.shape[0]
    w = DIFF_QK
    nt = s_len // tm
    tab_spec = pl.BlockSpec((tm, LANES), lambda i: (i % nt, 0))
    row = lambda cb: pl.BlockSpec((tm, w), lambda i: (i, cb))
    col = pl.BlockSpec((w, tm), lambda i: (0, i))
    return pl.pallas_call(
        _rope_body,
        grid=(t // tm,),
        in_specs=[row(1), row(2), row(3), tab_spec, tab_spec, tab_spec],
        out_specs=[col, row(0), pl.BlockSpec((None, w, tm), lambda i: (i // nt, 0, i % nt)), col],
        out_shape=[jax.ShapeDtypeStruct((w, t), BF16), jax.ShapeDtypeStruct((t, w), BF16),
                   jax.ShapeDtypeStruct((nb, w, s_len), F32), jax.ShapeDtypeStruct((w, t), BF16)],
        compiler_params=_params("parallel"),
        name="rope",
    )(proj, proj, proj, *tables)


def _diff_lambda(lam_ref, lam_init):
    lv = lam_ref[...]
    a = jnp.sum(lv[0:1] * lv[1:2], axis=-1, keepdims=True)
    b = jnp.sum(lv[2:3] * lv[3:4], axis=-1, keepdims=True)
    return jnp.exp(a) - jnp.exp(b) + lam_init


def _subln_gate(o, sub_g, g, lam_init):
    r = lax.rsqrt(jnp.mean(o * o, axis=-1, keepdims=True) + EPS)
    return ((o * r) * sub_g) * (1.0 - lam_init) * _silu(g)


def _sublane_allreduce(x, op):
    for sh in (1, 2, 4):
        x = op(x, pltpu.roll(x, sh, 0))
    return x


def _diff_attn_body(qi_ref, ki_ref, qt_ref, k_ref, vt_ref, g_ref, lam_ref, sg_ref, o_ref,
                    m_ref, l_ref, acc_ref, *, t, lam_init):
    pidx = pl.program_id(1)
    qi = qi_ref[pidx]
    ki = ki_ref[pidx]
    ng = t // SUBLANES

    @pl.when(ki == 0)
    def _():
        m_ref[...] = jnp.full(m_ref.shape, NEG_BIG, F32)
        l_ref[...] = jnp.zeros(l_ref.shape, F32)
        acc_ref[...] = jnp.zeros(acc_ref.shape, F32)

    def head(h, masked):
        r0 = pl.multiple_of(h * LANES, LANES)
        qt = qt_ref[pl.ds(r0, LANES), :]
        k = k_ref[:, pl.ds(r0, LANES)]
        vt = vt_ref[pl.ds(r0, LANES), :]
        sub = lax.broadcasted_iota(jnp.int32, qt.shape, 0)
        zero = jnp.zeros_like(qt)
        s2 = [_dot(k, jnp.where(sub < DIFF_DK, qt, zero)), _dot(k, jnp.where(sub >= DIFF_DK, qt, zero))]
        for c in range(2):
            s = s2[c]
            if masked:
                kpos = lax.broadcasted_iota(jnp.int32, s.shape, 0)
                qpos = lax.broadcasted_iota(jnp.int32, s.shape, 1)
                s = jnp.where(kpos <= qpos, s, NEG_BIG)
            s3 = s.reshape(ng, SUBLANES, t)
            m_old = m_ref[h, c]
            m_new = jnp.maximum(m_old, _sublane_allreduce(jnp.max(s3, axis=0), jnp.maximum))
            alpha = jnp.exp(m_old - m_new)
            p3 = jnp.exp(s3 - m_new[None])
            l_ref[h, c] = alpha * l_ref[h, c] + jnp.sum(p3, axis=0)
            pv = _dot(vt, p3.reshape(t, t).astype(BF16))
            acc = acc_ref[h, c].reshape(DIFF_DV // SUBLANES, SUBLANES, t) * alpha[None]
            acc_ref[h, c] = acc.reshape(DIFF_DV, t) + pv
            m_ref[h, c] = m_new

    def finish(h):
        r0 = pl.multiple_of(h * LANES, LANES)
        lam = _diff_lambda(lam_ref, lam_init)
        parts = []
        for c in range(2):
            l = _sublane_allreduce(l_ref[h, c], jnp.add)
            parts.append(acc_ref[h, c].reshape(DIFF_DV // SUBLANES, SUBLANES, t) / l[None])
        o = (parts[0] - lam * parts[1]).reshape(DIFF_DV, t).T
        o_ref[:, pl.ds(r0, LANES)] = _subln_gate(o, sg_ref[...], g_ref[:, pl.ds(r0, LANES)], lam_init).astype(BF16)

    @pl.when(ki < qi)
    def _():
        def body(h, carry):
            head(h, False)
            return carry
        lax.fori_loop(0, DIFF_HEADS, body, 0, unroll=2)

    @pl.when(ki == qi)
    def _():
        def body(h, carry):
            head(h, True)
            finish(h)
            return carry
        lax.fori_loop(0, DIFF_HEADS, body, 0, unroll=2)


def _diff_attn_prompt(qt, kb, vt, proj, lam_vecs, sub_g, lam_init, nb, s_len, t):
    tt = kb.shape[0]
    nq = s_len // t
    pairs = [(i, j) for i in range(nq) for j in range(i + 1)]
    qi_tab = jnp.asarray([a for a, _ in pairs], jnp.int32)
    ki_tab = jnp.asarray([b for _, b in pairs], jnp.int32)
    g_cb = (SSD_INNER + 2 * DIFF_QK + DIFF_WIDTH) // DIFF_WIDTH
    w = DIFF_WIDTH
    grid_spec = pltpu.PrefetchScalarGridSpec(
        num_scalar_prefetch=2,
        grid=(nb, len(pairs)),
        in_specs=[pl.BlockSpec((w, t), lambda b, p, qtab, ktab: (0, b * nq + qtab[p])),
                  pl.BlockSpec((t, w), lambda b, p, qtab, ktab: (b * nq + ktab[p], 0)),
                  pl.BlockSpec((w, t), lambda b, p, qtab, ktab: (0, b * nq + ktab[p])),
                  pl.BlockSpec((t, w), lambda b, p, qtab, ktab: (b * nq + qtab[p], g_cb)),
                  pl.BlockSpec((4, DIFF_DK), lambda b, p, qtab, ktab: (0, 0)),
                  pl.BlockSpec((1, DIFF_DV), lambda b, p, qtab, ktab: (0, 0))],
        out_specs=pl.BlockSpec((t, w), lambda b, p, qtab, ktab: (b * nq + qtab[p], 0)),
        scratch_shapes=[pltpu.VMEM((DIFF_HEADS, 2, SUBLANES, t), F32), pltpu.VMEM((DIFF_HEADS, 2, SUBLANES, t), F32),
                        pltpu.VMEM((DIFF_HEADS, 2, DIFF_DV, t), F32)],
    )
    return pl.pallas_call(
        functools.partial(_diff_attn_body, t=t, lam_init=lam_init),
        grid_spec=grid_spec,
        out_shape=jax.ShapeDtypeStruct((tt, w), BF16),
        compiler_params=_params("parallel", "arbitrary"),
        name="diff_attn_prompt",
    )(qi_tab, ki_tab, qt, kb, vt, proj, lam_vecs, sub_g.reshape(1, DIFF_DV))


PAGES_PER_STEP = 4


def _diff_attn_sample_body(pt_ref, qt_ref, knt_ref, vn_ref, g_ref, *rest, lam_init):
    npg = PAGES_PER_STEP
    kt_refs = rest[:npg]
    v_refs = rest[npg:2 * npg]
    lam_ref, sg_ref, o_ref, qcol_ref, m_ref, l_ref, acc_ref = rest[2 * npg:]
    b = pl.program_id(0)
    p = pl.program_id(1)
    nreq = qt_ref.shape[1]
    shape4 = (DIFF_HEADS, 2, DIFF_DK, PAGE_SIZE)

    @pl.when(p == 0)
    def _():
        onehot = (lax.broadcasted_iota(jnp.int32, (nreq, LANES), 0) == b).astype(BF16)
        qcol = _dot(qt_ref[...], onehot)
        qcol_ref[...] = qcol
        kn_hi, kn_lo = _split2(knt_ref[...])
        kcol = _dot(kn_hi, onehot) + _dot(kn_lo, onehot)
        prod = (qcol * kcol).reshape(shape4)
        for c in range(2):
            m_ref[c] = jnp.sum(prod[:, c], axis=1)
            l_ref[c] = jnp.ones((DIFF_HEADS, LANES), F32)
            acc_ref[c] = vn_ref[...]

    lane = lax.broadcasted_iota(jnp.int32, (DIFF_HEADS, LANES), 1)
    sub_w = lax.broadcasted_iota(jnp.int32, (DIFF_HEADS, PAGE_SIZE * DIFF_HEADS), 0)
    lane_w = lax.broadcasted_iota(jnp.int32, (DIFF_HEADS, PAGE_SIZE * DIFF_HEADS), 1)
    own_head = sub_w == lane_w % DIFF_HEADS
    keys_per_block = LANES // DIFF_HEADS
    for i in range(npg):
        prod = (qcol_ref[...] * kt_refs[i][...]).reshape(shape4)
        v2 = v_refs[i][...].reshape(PAGE_SIZE * DIFF_HEADS, DIFF_DV).astype(BF16)
        ws, alphas = [], []
        for c in range(2):
            s = jnp.sum(prod[:, c], axis=1)
            m_old = m_ref[c]
            m_new = jnp.maximum(m_old, jnp.max(s, axis=-1, keepdims=True))
            alpha = jnp.exp(m_old - m_new)
            pr = jnp.exp(s - m_new)
            l_ref[c] = alpha * l_ref[c] + jnp.sum(pr, axis=-1, keepdims=True)
            m_ref[c] = m_new
            wexp = jnp.concatenate(
                [jnp.take_along_axis(pr, keys_per_block * j + lane // DIFF_HEADS, axis=1)
                 for j in range(DIFF_HEADS)], axis=1)
            ws.append(jnp.where(own_head, wexp, 0.0))
            alphas.append(alpha)
        pv = _dot(jnp.concatenate(ws, axis=0).astype(BF16), v2)
        for c in range(2):
            acc_ref[c] = alphas[c] * acc_ref[c] + pv[c * DIFF_HEADS:(c + 1) * DIFF_HEADS]

    @pl.when(p == pl.num_programs(1) - 1)
    def _():
        lam = _diff_lambda(lam_ref, lam_init)
        o = acc_ref[0] / l_ref[0] - lam * (acc_ref[1] / l_ref[1])
        o_ref[...] = _subln_gate(o, sg_ref[...], g_ref[...], lam_init).astype(BF16)


def _diff_attn_sample(qt, kn_t, proj, cache_kt, cache_v, n_pool, page_table, layer_a, lam_vecs, sub_g, lam_init):
    nreq = kn_t.shape[1]
    n_pages = page_table.shape[1]
    npg = PAGES_PER_STEP
    assert n_pages % npg == 0
    base = layer_a * n_pool
    proj3 = proj.reshape(nreq, proj.shape[1] // DIFF_DV, DIFF_DV)
    v_cb = (SSD_INNER + 2 * DIFF_QK) // DIFF_WIDTH
    head_rows = lambda cb: pl.BlockSpec((None, DIFF_HEADS, DIFF_DV), lambda b, p, pt: (b, cb, 0))
    const = lambda shape: pl.BlockSpec(shape, lambda b, p, pt: (0,) * len(shape))

    def page_spec(shape, i):
        zeros = (0,) * (len(shape) - 1)
        return pl.BlockSpec(shape, lambda b, p, pt: (base + pt[b * n_pages + p * npg + i],) + zeros)

    grid_spec = pltpu.PrefetchScalarGridSpec(
        num_scalar_prefetch=1,
        grid=(nreq, n_pages // npg),
        in_specs=([const((DIFF_QK, nreq)), const((DIFF_QK, nreq)), head_rows(v_cb), head_rows(v_cb + 1)]
                  + [page_spec((None, DIFF_QK, PAGE_SIZE), i) for i in range(npg)]
                  + [page_spec((None, PAGE_SIZE, DIFF_HEADS, DIFF_DV), i) for i in range(npg)]
                  + [const((4, DIFF_DK)), const((1, DIFF_DV))]),
        out_specs=head_rows(0),
        scratch_shapes=[pltpu.VMEM((DIFF_QK, LANES), F32), pltpu.VMEM((2, DIFF_HEADS, LANES), F32),
                        pltpu.VMEM((2, DIFF_HEADS, LANES), F32), pltpu.VMEM((2, DIFF_HEADS, DIFF_DV), F32)],
    )
    out = pl.pallas_call(
        functools.partial(_diff_attn_sample_body, lam_init=lam_init),
        grid_spec=grid_spec,
        out_shape=jax.ShapeDtypeStruct((nreq, DIFF_HEADS, DIFF_DV), BF16),
        compiler_params=_params("parallel", "arbitrary"),
        name="diff_attn_sample",
    )(page_table.reshape(-1), qt, kn_t, proj3, proj3, *([cache_kt] * npg), *([cache_v] * npg), lam_vecs,
      sub_g.reshape(1, DIFF_DV))
    return out.reshape(nreq, DIFF_WIDTH)


def _xattn_heads(xq, xg, mk, mv):
    outs = []
    for h in range(X_HEADS):
        sl = slice(h * X_HEADDIM, (h + 1) * X_HEADDIM)
        s = _dot_nt(xq[:, sl].astype(BF16), mk[:, sl].astype(BF16)) * (X_HEADDIM ** -0.5)
        e = jnp.exp(s - jnp.max(s, axis=-1, keepdims=True))
        pr = (e / jnp.sum(e, axis=-1, keepdims=True)).astype(BF16)
        outs.append(_dot(pr, mv[:, sl].astype(BF16)) * _silu(xg[:, sl]))
    return outs


def _xattn_prompt_body(xq_ref, xg_ref, mk_ref, mv_ref, o_ref):
    outs = _xattn_heads(xq_ref[...], xg_ref[...], mk_ref[...], mv_ref[...])
    for h in range(X_HEADS):
        o_ref[:, h * X_HEADDIM:(h + 1) * X_HEADDIM] = outs[h].astype(BF16)


def _xattn_prompt(proj, xq_cb, mem_kv, layer, nb, s_len, tq):
    t = proj.shape[0]
    nq = s_len // tq
    return pl.pallas_call(
        _xattn_prompt_body,
        grid=(nb, nq),
        in_specs=[pl.BlockSpec((tq, X_WIDTH), lambda b, i: (b * nq + i, xq_cb)),
                  pl.BlockSpec((tq, X_WIDTH), lambda b, i: (b * nq + i, xq_cb + 1)),
                  pl.BlockSpec((None, N_MEM, X_WIDTH), lambda b, i: (b, 0, 2 * layer)),
                  pl.BlockSpec((None, N_MEM, X_WIDTH), lambda b, i: (b, 0, 2 * layer + 1))],
        out_specs=pl.BlockSpec((tq, X_WIDTH), lambda b, i: (b * nq + i, 0)),
        out_shape=jax.ShapeDtypeStruct((t, X_WIDTH), BF16),
        compiler_params=_params("parallel", "parallel"),
        name="xattn_prompt",
    )(proj, proj, mem_kv, mem_kv)


def _xattn_sample_body(xq_ref, xg_ref, mk_ref, mv_ref, o_ref, *, rb):
    for r in range(rb):
        xq = jnp.broadcast_to(xq_ref[r], (SUBLANES, X_WIDTH))
        xg = jnp.broadcast_to(xg_ref[r], (SUBLANES, X_WIDTH))
        outs = _xattn_heads(xq, xg, mk_ref[r], mv_ref[r])
        for h in range(X_HEADS):
            o_ref[r, :, h * X_HEADDIM:(h + 1) * X_HEADDIM] = outs[h][0:1].astype(BF16)


def _xattn_sample(proj, xq_cb, mem_k, mem_v, rb):
    nreq = proj.shape[0]
    p3 = proj.reshape(nreq, 1, proj.shape[1])
    out = pl.pallas_call(
        functools.partial(_xattn_sample_body, rb=rb),
        grid=(nreq // rb,),
        in_specs=[pl.BlockSpec((rb, 1, X_WIDTH), lambda i: (i, 0, xq_cb)),
                  pl.BlockSpec((rb, 1, X_WIDTH), lambda i: (i, 0, xq_cb + 1)),
                  pl.BlockSpec((rb, N_MEM, X_WIDTH), lambda i: (i, 0, 0)),
                  pl.BlockSpec((rb, N_MEM, X_WIDTH), lambda i: (i, 0, 0))],
        out_specs=pl.BlockSpec((rb, 1, X_WIDTH), lambda i: (i, 0, 0)),
        out_shape=jax.ShapeDtypeStruct((nreq, 1, X_WIDTH), BF16),
        compiler_params=_params("parallel"),
        name="xattn_sample",
    )(p3, p3, mem_k, mem_v)
    return out.reshape(nreq, X_WIDTH)


def _ssd_consts(dt_bias, a_log, d_skip):
    pad = LANES - SSD_HEADS
    dtb = jnp.pad(dt_bias.astype(F32), (0, pad)).reshape(1, LANES)
    alog = jnp.pad(a_log.astype(F32), (0, pad)).reshape(1, LANES)
    dexp = jnp.repeat(d_skip.astype(F32), SSD_HEADDIM).reshape(1, SSD_INNER)
    head = np.arange(LANES)[:, None]
    chan_head = np.arange(SSD_INNER)[None, :] // SSD_HEADDIM
    eexp = jnp.asarray((head == chan_head).astype(np.float32), BF16)
    return dtb, alog, dexp, eexp


def _softplus(x):
    return jnp.maximum(x, 0.0) + jnp.log1p(jnp.exp(-jnp.abs(x)))


def _ssd_gate_norm(y, xs, z, dexp, ng):
    y = (y + dexp * xs) * _silu(z)
    r = lax.rsqrt(jnp.mean(y * y, axis=-1, keepdims=True) + EPS)
    return ((y * r) * ng).astype(BF16)


def _ssd_prompt_body(z_ref, xbc_ref, dtr_ref, cw_ref, cb_ref, dtb_ref, alog_ref, dexp_ref, ng_ref, eexp_ref,
                     y_ref, hout_ref, ext_ref, h_ref, yd_ref):
    q = SSD_CHUNK
    c = pl.program_id(1)

    @pl.when(c == 0)
    def _():
        ext_ref[0:SUBLANES, :] = jnp.zeros((SUBLANES, SSD_CONV_CH), F32)
        h_ref[...] = jnp.zeros(h_ref.shape, F32)

    x_raw = xbc_ref[...]
    ext_ref[SUBLANES:SUBLANES + q, :] = x_raw
    conv = cb_ref[...] + cw_ref[SSD_CONV - 1:SSD_CONV, :] * x_raw
    for k in range(1, SSD_CONV):
        conv = conv + cw_ref[SSD_CONV - 1 - k:SSD_CONV - k, :] * ext_ref[SUBLANES - k:SUBLANES - k + q, :]
    ext_ref[0:SUBLANES, :] = x_raw[q - SUBLANES:q, :]
    xbc = _silu(conv)
    xs = xbc[:, :SSD_INNER]
    bm = xbc[:, SSD_INNER:SSD_INNER + SSD_GROUPS * SSD_STATE].astype(BF16)
    cm = xbc[:, SSD_INNER + SSD_GROUPS * SSD_STATE:].astype(BF16)

    eexp = eexp_ref[...]
    dt = _softplus(dtr_ref[...] + dtb_ref[...])
    da = dt * (-jnp.exp(alog_ref[...]))
    ri = lax.broadcasted_iota(jnp.int32, (q, q), 0)
    ci = lax.broadcasted_iota(jnp.int32, (q, q), 1)
    tri = ri >= ci
    tri_b = tri.astype(BF16)
    hi, mid, lo = _split3(da)
    acum = _dot(tri_b, hi) + _dot(tri_b, mid) + _dot(tri_b, lo)
    acum_t = acum.T
    a_last = acum[q - 1:q, :]
    xdt = xs * _expand(_split2(dt), eexp)
    xdt_b = xdt.astype(BF16)
    ea_exp = _expand(_split2(jnp.exp(acum)), eexp)
    wend_exp = _expand(_split2(jnp.exp(a_last - acum)), eexp)
    xw_t = (xdt * wend_exp).T.astype(BF16)
    decay = jnp.exp(a_last)

    lane = lax.broadcasted_iota(jnp.int32, (q, LANES), 1)
    for g in range(SSD_GROUPS):
        gs = slice(g * SSD_STATE, (g + 1) * SSD_STATE)
        ch = slice(g * SSD_HPG * SSD_HEADDIM, (g + 1) * SSD_HPG * SSD_HEADDIM)
        scores = _dot_nt(cm[:, gs], bm[:, gs])
        for pair in range(SSD_HPG // 2):
            halves = []
            for r in range(2):
                hd = g * SSD_HPG + 2 * pair + r
                seg = acum[:, hd:hd + 1] - acum_t[hd:hd + 1, :]
                ldec = jnp.where(tri, jnp.exp(jnp.where(tri, seg, 0.0)), 0.0)
                xp = xdt_b[:, (hd // 2) * LANES:(hd // 2 + 1) * LANES]
                halves.append(_dot((scores * ldec).astype(BF16), xp))
            lo_lane = (g * SSD_HPG + 2 * pair) // 2 * LANES
            yd_ref[:, lo_lane:lo_lane + LANES] = jnp.where(lane < SSD_HEADDIM, halves[0], halves[1])
        h_g = h_ref[ch, :]
        yd_ref[:, ch] = yd_ref[:, ch] + _dot_nt(cm[:, gs], h_g.astype(BF16)) * ea_exp[:, ch]
        upd = _dot(xw_t[ch, :], bm[:, gs])
        for r in range(SSD_HPG):
            hd = g * SSD_HPG + r
            rows = slice(hd * SSD_HEADDIM, (hd + 1) * SSD_HEADDIM)
            h_ref[rows, :] = h_ref[rows, :] * decay[:, hd:hd + 1] + upd[r * SSD_HEADDIM:(r + 1) * SSD_HEADDIM, :]

    y_ref[...] = _ssd_gate_norm(yd_ref[...], xs, z_ref[...], dexp_ref[...], ng_ref[...])
    hout_ref[...] = h_ref[...]


def _ssd_prompt(proj, proj_s, conv_w, conv_b, consts, norm_g, nb, s_len):
    t = proj.shape[0]
    q = SSD_CHUNK
    nc = s_len // q
    dtb, alog, dexp, eexp = consts
    const = lambda shape: pl.BlockSpec(shape, lambda b, c: (0,) * len(shape))
    return pl.pallas_call(
        _ssd_prompt_body,
        grid=(nb, nc),
        in_specs=[pl.BlockSpec((q, SSD_INNER), lambda b, c: (b * nc + c, 0)),
                  pl.BlockSpec((q, SSD_CONV_CH), lambda b, c: (b * nc + c, 0)),
                  pl.BlockSpec((q, LANES), lambda b, c: (b * nc + c, SSD_CONV_CH // LANES)),
                  const((SSD_CONV, SSD_CONV_CH)), const((1, SSD_CONV_CH)), const((1, LANES)), const((1, LANES)),
                  const((1, SSD_INNER)), const((1, SSD_INNER)), const((LANES, SSD_INNER))],
        out_specs=[pl.BlockSpec((q, SSD_INNER), lambda b, c: (b * nc + c, 0)),
                   pl.BlockSpec((None, SSD_INNER, SSD_STATE), lambda b, c: (b, 0, 0))],
        out_shape=[jax.ShapeDtypeStruct((t, SSD_INNER), BF16),
                   jax.ShapeDtypeStruct((nb, SSD_INNER, SSD_STATE), F32)],
        scratch_shapes=[pltpu.VMEM((SUBLANES + q, SSD_CONV_CH), F32), pltpu.VMEM((SSD_INNER, SSD_STATE), F32),
                        pltpu.VMEM((q, SSD_INNER), F32)],
        compiler_params=_params("parallel", "arbitrary"),
        name="ssd_prompt",
    )(proj, proj_s, proj_s, conv_w, conv_b.reshape(1, -1), dtb, alog, dexp, norm_g.reshape(1, -1), eexp)


def _ssd_sample_body(z_ref, xbc_ref, dtr_ref, cs_ref, hin_ref, cw_ref, cb_ref, dtb_ref, alog_ref, dexp_ref, ng_ref,
                     eexp_ref, y_ref, hout_ref, xs_ref, xdt_t_ref, da_hi_ref, da_lo_ref, bm_ref, cm_t_ref, y_t_ref,
                     *, rb):
    i = pl.program_id(0)
    nreq = z_ref.shape[0]

    @pl.when(i == 0)
    def _():
        conv = cb_ref[...] + cw_ref[SSD_CONV - 1:SSD_CONV, :] * xbc_ref[...]
        for k in range(SSD_CONV - 1):
            conv = conv + cw_ref[k:k + 1, :] * cs_ref[k]
        xbc = _silu(conv)
        xs = xbc[:, :SSD_INNER]
        xs_ref[...] = xs
        bm_ref[...] = xbc[:, SSD_INNER:SSD_INNER + SSD_GROUPS * SSD_STATE]
        cm_t_ref[...] = xbc[:, SSD_INNER + SSD_GROUPS * SSD_STATE:].T
        dt = _softplus(dtr_ref[...] + dtb_ref[...])
        da_hi, da_lo = _split2(jnp.exp(dt * (-jnp.exp(alog_ref[...]))).T)
        da_hi_ref[...] = da_hi
        da_lo_ref[...] = da_lo
        xdt_t_ref[...] = (xs * _expand(_split2(dt), eexp_ref[...])).T.astype(BF16)
        y_t_ref[...] = jnp.zeros(y_t_ref.shape, F32)

    def per_request(r, carry):
        b = i * rb + r
        onehot = (lax.broadcasted_iota(jnp.int32, (nreq, LANES), 0) == b).astype(BF16)
        xb = _dot(xdt_t_ref[...], onehot)
        dab = _dot(da_hi_ref[...], onehot) + _dot(da_lo_ref[...], onehot)
        brow = bm_ref[pl.ds(b, 1), :]
        cmask = lax.broadcasted_iota(jnp.int32, (SSD_STATE, nreq), 1) == b
        for g in range(SSD_GROUPS):
            gs = slice(g * SSD_STATE, (g + 1) * SSD_STATE)
            ch = slice(g * SSD_HPG * SSD_HEADDIM, (g + 1) * SSD_HPG * SSD_HEADDIM)
            for hr in range(SSD_HPG):
                hd = g * SSD_HPG + hr
                rows = slice(hd * SSD_HEADDIM, (hd + 1) * SSD_HEADDIM)
                hout_ref[r, rows, :] = dab[hd:hd + 1, :] * hin_ref[r, rows, :] + xb[rows, :] * brow[:, gs]
            cmat = jnp.where(cmask, cm_t_ref[gs, :], 0.0).astype(BF16)
            y_t_ref[ch, :] = y_t_ref[ch, :] + _dot(hout_ref[r, ch, :].astype(BF16), cmat)
        return carry

    lax.fori_loop(0, rb, per_request, 0)

    @pl.when(i == pl.num_programs(0) - 1)
    def _():
        y_ref[...] = _ssd_gate_norm(y_t_ref[...].T, xs_ref[...], z_ref[...], dexp_ref[...], ng_ref[...])


def _ssd_sample(proj, proj_s, conv_state_t, h_state, layer_a, conv_w, conv_b, consts, norm_g, rb):
    nreq = proj.shape[0]
    dtb, alog, dexp, eexp = consts
    const = lambda shape: pl.BlockSpec(shape, lambda i: (0,) * len(shape))
    return pl.pallas_call(
        functools.partial(_ssd_sample_body, rb=rb),
        grid=(nreq // rb,),
        in_specs=[pl.BlockSpec((nreq, SSD_INNER), lambda i: (0, 0)),
                  pl.BlockSpec((nreq, SSD_CONV_CH), lambda i: (0, 0)),
                  pl.BlockSpec((nreq, LANES), lambda i: (0, SSD_CONV_CH // LANES)),
                  const((SSD_CONV - 1, nreq, SSD_CONV_CH)),
                  pl.BlockSpec((None, rb, SSD_INNER, SSD_STATE), lambda i: (layer_a, i, 0, 0)),
                  const((SSD_CONV, SSD_CONV_CH)), const((1, SSD_CONV_CH)), const((1, LANES)), const((1, LANES)),
                  const((1, SSD_INNER)), const((1, SSD_INNER)), const((LANES, SSD_INNER))],
        out_specs=[const((nreq, SSD_INNER)),
                   pl.BlockSpec((rb, SSD_INNER, SSD_STATE), lambda i: (i, 0, 0))],
        out_shape=[jax.ShapeDtypeStruct((nreq, SSD_INNER), BF16),
                   jax.ShapeDtypeStruct(h_state.shape[1:], F32)],
        scratch_shapes=[pltpu.VMEM((nreq, SSD_INNER), F32), pltpu.VMEM((SSD_INNER, nreq), BF16),
                        pltpu.VMEM((LANES, nreq), BF16), pltpu.VMEM((LANES, nreq), BF16),
                        pltpu.VMEM((nreq, SSD_GROUPS * SSD_STATE), F32),
                        pltpu.VMEM((SSD_GROUPS * SSD_STATE, nreq), F32), pltpu.VMEM((SSD_INNER, nreq), F32)],
        compiler_params=_params("arbitrary"),
        name="ssd_sample",
    )(proj, proj_s, proj_s, conv_state_t, h_state, conv_w, conv_b.reshape(1, -1), dtb, alog, dexp,
      norm_g.reshape(1, -1), eexp)


CONF_PAD = 32
CONF_RB = 64
CONF_CB = LANES
CONF_NCB = CONF_WIDTH // CONF_CB


def _conf_finish(c, g, ln_g, ln_b):
    mu = jnp.mean(c, axis=-1, keepdims=True)
    xc = c - mu
    r = lax.rsqrt(jnp.mean(xc * xc, axis=-1, keepdims=True) + EPS)
    return (_silu((xc * r) * ln_g + ln_b) * _silu(g)).astype(BF16)


def _conf_prompt_body(a_ref, ag_ref, g_ref, w_ref, b_ref, lg_ref, lb_ref, y_ref, tail_ref, sh_ref, c_ref, *, tt):
    i = pl.program_id(1)
    rows = CONF_PAD + tt

    @pl.when(i == 0)
    def _():
        sh_ref[0, :, 0:CONF_PAD, :] = jnp.zeros((CONF_NCB, CONF_PAD, CONF_CB), F32)

    a = a_ref[...]
    ag = ag_ref[...]
    u = a / (1.0 + jnp.exp(-ag))
    tail_ref[...] = u[tt - CONF_PAD:tt, :]
    for cb in range(CONF_NCB):
        sh_ref[0, cb, CONF_PAD:rows, :] = u[:, cb * CONF_CB:(cb + 1) * CONF_CB]
        for s in range(1, SUBLANES):
            sh_ref[s, cb, 0:rows - SUBLANES, :] = sh_ref[0, cb, s:s + rows - SUBLANES, :]

    first = CONF_PAD - (CONF_CONV_W - 1)

    def tile(idx, carry):
        rb = idx // CONF_NCB
        cb = idx % CONF_NCB
        r0 = pl.multiple_of(rb * CONF_RB, CONF_RB)
        acc = jnp.broadcast_to(b_ref[cb], (CONF_RB, CONF_CB))
        for k in range(CONF_CONV_W):
            off = first + k
            src = sh_ref[off % SUBLANES, cb, pl.ds(r0 + (off // SUBLANES) * SUBLANES, CONF_RB), :]
            acc = acc + w_ref[k, cb] * src
        c_ref[cb, pl.ds(r0, CONF_RB), :] = acc
        return carry

    lax.fori_loop(0, (tt // CONF_RB) * CONF_NCB, tile, 0)
    for cb in range(CONF_NCB):
        sh_ref[0, cb, 0:CONF_PAD, :] = sh_ref[0, cb, tt:rows, :]
    c = jnp.concatenate([c_ref[cb] for cb in range(CONF_NCB)], axis=1)
    y_ref[...] = _conf_finish(c, g_ref[...], lg_ref[...], lb_ref[...])


def _conf_prompt(proj, dw_w, dw_b, ln_g, ln_b, nb, s_len, tt):
    t = proj.shape[0]
    nt = s_len // tt
    col = lambda cb: pl.BlockSpec((tt, CONF_WIDTH), lambda b, i: (b * nt + i, cb))
    const = lambda shape: pl.BlockSpec(shape, lambda b, i: (0,) * len(shape))
    return pl.pallas_call(
        functools.partial(_conf_prompt_body, tt=tt),
        grid=(nb, nt),
        in_specs=[col(0), col(1), col(2), const((CONF_CONV_W, CONF_NCB, 1, CONF_CB)), const((CONF_NCB, 1, CONF_CB)),
                  const((1, CONF_WIDTH)), const((1, CONF_WIDTH))],
        out_specs=[pl.BlockSpec((tt, CONF_WIDTH), lambda b, i: (b * nt + i, 0)),
                   pl.BlockSpec((None, CONF_PAD, CONF_WIDTH), lambda b, i: (b, 0, 0))],
        out_shape=[jax.ShapeDtypeStruct((t, CONF_WIDTH), BF16),
                   jax.ShapeDtypeStruct((nb, CONF_PAD, CONF_WIDTH), F32)],
        scratch_shapes=[pltpu.VMEM((SUBLANES, CONF_NCB, CONF_PAD + tt, CONF_CB), F32),
                        pltpu.VMEM((CONF_NCB, tt, CONF_CB), F32)],
        compiler_params=_params("parallel", "arbitrary"),
        name="conf_prompt",
    )(proj, proj, proj, dw_w.reshape(CONF_CONV_W, CONF_NCB, 1, CONF_CB), dw_b.reshape(CONF_NCB, 1, CONF_CB),
      ln_g.reshape(1, -1), ln_b.reshape(1, -1))


def _conf_sample_body(a_ref, ag_ref, g_ref, st_ref, w_ref, b_ref, lg_ref, lb_ref, y_ref, u_ref, c_ref, *, tc):
    j = pl.program_id(0)
    u = a_ref[...] / (1.0 + jnp.exp(-ag_ref[...]))
    u_ref[...] = u
    acc = b_ref[...] + w_ref[CONF_CONV_W - 1:CONF_CONV_W, :] * u
    for k in range(CONF_CONV_W - 1):
        acc = acc + w_ref[k:k + 1, :] * st_ref[k]
    c_ref[j] = acc

    @pl.when(j == pl.num_programs(0) - 1)
    def _():
        c = jnp.concatenate([c_ref[k] for k in range(CONF_WIDTH // tc)], axis=1)
        y_ref[...] = _conf_finish(c, g_ref[...], lg_ref[...], lb_ref[...])


def _conf_sample(proj, state_t, dw_w, dw_b, ln_g, ln_b, tc):
    nreq = proj.shape[0]
    nblk = CONF_WIDTH // tc
    col = lambda off: pl.BlockSpec((nreq, tc), lambda j: (0, off * nblk + j))
    vec = pl.BlockSpec((1, tc), lambda j: (0, j))
    full = pl.BlockSpec((1, CONF_WIDTH), lambda j: (0, 0))
    return pl.pallas_call(
        functools.partial(_conf_sample_body, tc=tc),
        grid=(nblk,),
        in_specs=[col(0), col(1), pl.BlockSpec((nreq, CONF_WIDTH), lambda j: (0, 2)),
                  pl.BlockSpec((CONF_CONV_W - 1, nreq, tc), lambda j: (0, 0, j)),
                  pl.BlockSpec((CONF_CONV_W, tc), lambda j: (0, j)), vec, full, full],
        out_specs=[pl.BlockSpec((nreq, CONF_WIDTH), lambda j: (0, 0)),
                   pl.BlockSpec((nreq, tc), lambda j: (0, j))],
        out_shape=[jax.ShapeDtypeStruct((nreq, CONF_WIDTH), BF16),
                   jax.ShapeDtypeStruct((nreq, CONF_WIDTH), F32)],
        scratch_shapes=[pltpu.VMEM((nblk, nreq, tc), F32)],
        compiler_params=_params("arbitrary"),
        name="conf_sample",
    )(proj, proj, proj, state_t, dw_w, dw_b.reshape(1, -1), ln_g.reshape(1, -1), ln_b.reshape(1, -1))


def _even_weights(w_in):
    cuts = np.cumsum([SSD_INNER, SSD_CONV_CH, SSD_HEADS]).tolist()
    z, xbc, dt, rest = (w_in[:, :cuts[0]], w_in[:, cuts[0]:cuts[1]], w_in[:, cuts[1]:cuts[2]], w_in[:, cuts[2]:])
    main = jnp.concatenate([z, rest], axis=1).astype(BF16)
    pad = jnp.zeros((w_in.shape[0], LANES - SSD_HEADS), w_in.dtype)
    ssd = jnp.concatenate([xbc, dt, pad], axis=1).astype(BF16)
    return main, ssd


def kernel(x_prompt, x_sample, cache_attn_k, cache_attn_v, cache_mem_k, cache_mem_v, state_ssm, state_ssm_conv,
           state_conf_conv, page_table, mem_prompt, norm_a, w_in_a, ssm_conv_w, ssm_conv_b, ssm_dt_bias, ssm_a_log,
           ssm_d, ssm_norm, diff_lambda, diff_subln, w_out_a, norm_c, w_in_c, conf_dw_w, conf_dw_b, conf_ln_g,
           conf_ln_b, w_out_c, xattn_wk, xattn_wv, mem_norm, norm_f):
    bp, sp, d = x_prompt.shape
    bs = x_sample.shape[0]
    n_a = w_in_a.shape[0]
    n_pool = cache_attn_k.shape[1]
    past = page_table.shape[1] * PAGE_SIZE
    tp = bp * sp

    tm_p = min(1024, sp)
    t_attn = min(512, sp)
    tm_rope = min(512, sp)
    tq_x = min(512, sp)
    tt_conf = min(256, sp)
    tm_out = min(512, sp)

    xp = x_prompt.reshape(tp, d)
    xs = x_sample.reshape(bs, d)

    w_mem = jnp.concatenate([jnp.concatenate([xattn_wk[l], xattn_wv[l]], axis=1) for l in range(DEPTH)],
                            axis=1).astype(BF16)
    mem_kv = _norm_matmul(mem_prompt.reshape(bp * N_MEM, d), mem_norm, w_mem, N_MEM, 1024)
    mem_kv3 = mem_kv.reshape(bp, N_MEM, DEPTH * 2 * X_WIDTH)
    mem_kv5 = mem_kv.reshape(bp, N_MEM, DEPTH, 2, X_HEADS, X_HEADDIM)
    p_mem_k = jnp.moveaxis(mem_kv5[:, :, :, 0], 2, 0)
    p_mem_v = jnp.moveaxis(mem_kv5[:, :, :, 1], 2, 0)

    tab_p = _rope_tables(jnp.arange(sp))
    tab_s = _rope_tables(jnp.full((bs,), past, jnp.int32))
    cache_kt = jnp.swapaxes(cache_attn_k.reshape(n_a * n_pool, PAGE_SIZE, DIFF_QK), 1, 2)
    cache_v4 = cache_attn_v.reshape(n_a * n_pool, PAGE_SIZE, DIFF_HEADS, DIFF_DV)
    h_state = state_ssm.reshape(n_a, bs, SSD_INNER, SSD_STATE)

    pk, pv, pssm, pconv, pconf = [], [], [], [], []
    sk, sv, sssm, sconv, sconf = [], [], [], [], []
    for layer in range(DEPTH):
        j = layer // 2
        final = layer == DEPTH - 1
        mem_k_s = cache_mem_k[layer].reshape(bs, N_MEM, X_WIDTH)
        mem_v_s = cache_mem_v[layer].reshape(bs, N_MEM, X_WIDTH)
        if layer % 2 == 0:
            lam_init = 0.8 - 0.6 * math.exp(-0.3 * layer)
            w_main, w_ssd = _even_weights(w_in_a[j])
            w_out = w_out_a[j].astype(BF16)
            consts = _ssd_consts(ssm_dt_bias[j], ssm_a_log[j], ssm_d[j])
            xq_cb = (SSD_INNER + 2 * DIFF_QK + 2 * DIFF_WIDTH) // X_WIDTH
            v_lo = SSD_INNER + 2 * DIFF_QK

            proj = _norm_matmul(xp, norm_a[j], w_main, tm_p, 1024)
            proj_s = _norm_matmul(xp, norm_a[j], w_ssd, tm_p, w_ssd.shape[1])
            y_ssd, h_last = _ssd_prompt(proj, proj_s, ssm_conv_w[j], ssm_conv_b[j], consts, ssm_norm[j], bp, sp)
            qt, kb, k_rot_t, vt = _rope(proj, tab_p, bp, sp, tm_rope)
            y_diff = _diff_attn_prompt(qt, kb, vt, proj, diff_lambda[j], diff_subln[j], lam_init, bp, sp, t_attn)
            y_x = _xattn_prompt(proj, xq_cb, mem_kv3, layer, bp, sp, tq_x)
            xp = _out_proj([y_ssd, y_diff, y_x], w_out, xp, norm_f, final, tm_out)
            pk.append(k_rot_t)
            pv.append(proj[:, v_lo:v_lo + DIFF_WIDTH].reshape(bp, sp, DIFF_HEADS, DIFF_DV))
            pssm.append(h_last.reshape(bp, SSD_GROUPS, SSD_HPG, SSD_HEADDIM, SSD_STATE))
            pconv.append(proj_s.reshape(bp, sp, -1)[:, sp - (SSD_CONV - 1):, :SSD_CONV_CH])

            proj = _norm_matmul(xs, norm_a[j], w_main, bs, 1024)
            proj_s = _norm_matmul(xs, norm_a[j], w_ssd, bs, w_ssd.shape[1])
            conv_t = jnp.swapaxes(state_ssm_conv[j], 0, 1)
            y_ssd, h_new = _ssd_sample(proj, proj_s, conv_t, h_state, j, ssm_conv_w[j], ssm_conv_b[j], consts,
                                       ssm_norm[j], 8)
            qt, _, k_rot_t, _ = _rope(proj, tab_s, 1, bs, bs)
            y_diff = _diff_attn_sample(qt, k_rot_t[0], proj, cache_kt, cache_v4, n_pool, page_table, j, diff_lambda[j],
                                       diff_subln[j], lam_init)
            y_x = _xattn_sample(proj, xq_cb, mem_k_s, mem_v_s, 4)
            xs = _out_proj([y_ssd, y_diff, y_x], w_out, xs, norm_f, final, bs)
            sk.append(k_rot_t[0].T.reshape(bs, 1, DIFF_HEADS, 2, DIFF_DK))
            sv.append(proj[:, v_lo:v_lo + DIFF_WIDTH].reshape(bs, 1, DIFF_HEADS, DIFF_DV))
            sssm.append(h_new.reshape(bs, SSD_GROUPS, SSD_HPG, SSD_HEADDIM, SSD_STATE))
            sconv.append(jnp.concatenate([state_ssm_conv[j][:, 1:], proj_s[:, None, :SSD_CONV_CH]], axis=1))
        else:
            w_in = w_in_c[j].astype(BF16)
            w_out = w_out_c[j].astype(BF16)
            xq_cb = 3 * CONF_WIDTH // X_WIDTH

            proj = _norm_matmul(xp, norm_c[j], w_in, tm_p, 1024)
            y_conf, tail = _conf_prompt(proj, conf_dw_w[j], conf_dw_b[j], conf_ln_g[j], conf_ln_b[j], bp, sp, tt_conf)
            y_x = _xattn_prompt(proj, xq_cb, mem_kv3, layer, bp, sp, tq_x)
            xp = _out_proj([y_conf, y_x], w_out, xp, norm_f, final, tm_out)
            pconf.append(tail[:, CONF_PAD - (CONF_CONV_W - 1):])

            proj = _norm_matmul(xs, norm_c[j], w_in, bs, 1024)
            st_t = jnp.swapaxes(state_conf_conv[j], 0, 1)
            y_conf, u = _conf_sample(proj, st_t, conf_dw_w[j], conf_dw_b[j], conf_ln_g[j], conf_ln_b[j], 256)
            y_x = _xattn_sample(proj, xq_cb, mem_k_s, mem_v_s, 4)
            xs = _out_proj([y_conf, y_x], w_out, xs, norm_f, final, bs)
            sconf.append(jnp.concatenate([state_conf_conv[j][:, 1:], u[:, None]], axis=1))

    p_attn_k = jnp.transpose(jnp.stack(pk).reshape(n_a, bp, DIFF_HEADS, 2, DIFF_DK, sp), (0, 1, 5, 2, 3, 4))
    return (xp.reshape(bp, sp, d), xs.reshape(bs, 1, d), p_attn_k, jnp.stack(pv), jnp.stack(pssm),
            jnp.stack(pconv), jnp.stack(pconf), p_mem_k, p_mem_v, jnp.stack(sk), jnp.stack(sv), jnp.stack(sssm),
            jnp.stack(sconv), jnp.stack(sconf))
```

```python
import functools
import math

import jax
import jax.numpy as jnp
import numpy as np
from jax import lax
from jax.experimental import pallas as pl
from jax.experimental.pallas import tpu as pltpu

F32 = jnp.float32
BF16 = jnp.bfloat16

D_MODEL = 1024
DEPTH = 4
PAGE_SIZE = 128
SSD_INNER = 1024
SSD_HEADDIM = 64
SSD_HEADS = 16
SSD_GROUPS = 2
SSD_HPG = 8
SSD_STATE = 128
SSD_CONV = 4
SSD_CHUNK = 128
SSD_CONV_CH = 1536
DIFF_HEADS = 8
DIFF_DK = 64
DIFF_DV = 128
DIFF_QK = 1024
DIFF_WIDTH = 1024
ROT_DIM = 16
ROPE_THETA = 500000.0
CONF_WIDTH = 2048
CONF_CONV_W = 31
N_MEM = 256
X_HEADS = 4
X_HEADDIM = 128
X_WIDTH = 512
EPS = 1e-6

LANES = 128
SUBLANES = 8
VMEM_LIMIT_BYTES = 56 * 1024 * 1024
NEG_BIG = -1e30


def _params(*sem):
    return pltpu.CompilerParams(dimension_semantics=sem, vmem_limit_bytes=VMEM_LIMIT_BYTES)


def _silu(x):
    return x / (1.0 + jnp.exp(-x))


def _split2(x):
    hi = x.astype(BF16)
    lo = (x - hi.astype(F32)).astype(BF16)
    return hi, lo


def _split3(x):
    hi = x.astype(BF16)
    r = x - hi.astype(F32)
    mid = r.astype(BF16)
    lo = (r - mid.astype(F32)).astype(BF16)
    return hi, mid, lo


def _dot(a, b):
    return jnp.dot(a, b, preferred_element_type=F32)


def _dot_nt(a, b):
    return lax.dot_general(a, b, (((1,), (1,)), ((), ())), preferred_element_type=F32)


def _expand(parts, e):
    acc = _dot(parts[0], e)
    for p in parts[1:]:
        acc = acc + _dot(p, e)
    return acc


def _norm_matmul_body(x_ref, g_ref, w_ref, o_ref, h_ref):
    @pl.when(pl.program_id(1) == 0)
    def _():
        x = x_ref[...]
        r = lax.rsqrt(jnp.mean(x * x, axis=-1, keepdims=True) + EPS)
        h_ref[...] = ((x * r) * g_ref[...]).astype(BF16)

    o_ref[...] = _dot(h_ref[...], w_ref[...])


def _norm_matmul(x, g, w, tm, tn):
    m, d = x.shape
    n = w.shape[1]
    return pl.pallas_call(
        _norm_matmul_body,
        grid=(m // tm, n // tn),
        in_specs=[pl.BlockSpec((tm, d), lambda i, j: (i, 0)),
                  pl.BlockSpec((1, d), lambda i, j: (0, 0)),
                  pl.BlockSpec((d, tn), lambda i, j: (0, j))],
        out_specs=pl.BlockSpec((tm, tn), lambda i, j: (i, j)),
        out_shape=jax.ShapeDtypeStruct((m, n), F32),
        scratch_shapes=[pltpu.VMEM((tm, d), BF16)],
        compiler_params=_params("parallel", "arbitrary"),
        name="norm_matmul",
    )(x, g.reshape(1, d), w)


def _out_proj_body(*refs, n_in, final):
    ins = refs[:n_in]
    ws = refs[n_in:2 * n_in]
    x_ref = refs[2 * n_in]
    g_ref = refs[2 * n_in + 1]
    o_ref = refs[2 * n_in + 2]
    acc = x_ref[...]
    for a, w in zip(ins, ws):
        acc = acc + _dot(a[...], w[...])
    if final:
        r = lax.rsqrt(jnp.mean(acc * acc, axis=-1, keepdims=True) + EPS)
        acc = (acc * r) * g_ref[...]
    o_ref[...] = acc


def _out_proj(parts, w, x, norm_f, final, tm):
    m, d = x.shape
    widths = [p.shape[1] for p in parts]
    offs = np.cumsum([0] + widths).tolist()
    ws = [w[offs[i]:offs[i + 1]] for i in range(len(parts))]
    in_specs = ([pl.BlockSpec((tm, wd), lambda i: (i, 0)) for wd in widths]
                + [pl.BlockSpec((wd, d), lambda i: (0, 0)) for wd in widths]
                + [pl.BlockSpec((tm, d), lambda i: (i, 0)), pl.BlockSpec((1, d), lambda i: (0, 0))])
    return pl.pallas_call(
        functools.partial(_out_proj_body, n_in=len(parts), final=final),
        grid=(m // tm,),
        in_specs=in_specs,
        out_specs=pl.BlockSpec((tm, d), lambda i: (i, 0)),
        out_shape=jax.ShapeDtypeStruct((m, d), F32),
        compiler_params=_params("parallel"),
        name="out_proj",
    )(*parts, *ws, x, norm_f.reshape(1, d))


def _rope_tables(pos):
    half = ROT_DIM // 2
    inv_freq = ROPE_THETA ** (-jnp.arange(0, ROT_DIM, 2, dtype=F32) / ROT_DIM)
    ang = pos.astype(F32)[:, None] * inv_freq[None, :]
    cos, sin = jnp.cos(ang), jnp.sin(ang)
    n = pos.shape[0]
    zeros = jnp.zeros((n, DIFF_DK - ROT_DIM), F32)
    z8 = jnp.zeros((n, half), F32)
    c = jnp.concatenate([cos, cos, zeros + 1.0], axis=-1)
    sa = jnp.concatenate([-sin, z8, zeros], axis=-1)
    sb = jnp.concatenate([z8, sin, zeros], axis=-1)
    return (jnp.concatenate([c, c], axis=-1), jnp.concatenate([sa, sa], axis=-1),
            jnp.concatenate([sb, sb], axis=-1))


def _rope_body(q_ref, k_ref, v_ref, c_ref, sa_ref, sb_ref, qt_ref, kb_ref, kt_ref, vt_ref):
    c, sa, sb = c_ref[...], sa_ref[...], sb_ref[...]
    half = ROT_DIM // 2

    def rot(x):
        return x * c + pltpu.roll(x, LANES - half, 1) * sa + pltpu.roll(x, half, 1) * sb

    for h in range(DIFF_HEADS):
        sl = slice(h * LANES, (h + 1) * LANES)
        kr = rot(k_ref[:, sl])
        kb_ref[:, sl] = kr.astype(BF16)
        kt_ref[sl, :] = kr.T
        qt_ref[sl, :] = (rot(q_ref[:, sl]) * (DIFF_DK ** -0.5)).T.astype(BF16)
        vt_ref[sl, :] = v_ref[:, sl].T.astype(BF16)


def _rope(proj, tables, nb, s_len, tm):
    t = proj.shape[0]
    w = DIFF_QK
    nt = s_len // tm
    tab_spec = pl.BlockSpec((tm, LANES), lambda i: (i % nt, 0))
    row = lambda cb: pl.BlockSpec((tm, w), lambda i: (i, cb))
    col = pl.BlockSpec((w, tm), lambda i: (0, i))
    return pl.pallas_call(
        _rope_body,
        grid=(t // tm,),
        in_specs=[row(1), row(2), row(3), tab_spec, tab_spec, tab_spec],
        out_specs=[col, row(0), pl.BlockSpec((None, w, tm), lambda i: (i // nt, 0, i % nt)), col],
        out_shape=[jax.ShapeDtypeStruct((w, t), BF16), jax.ShapeDtypeStruct((t, w), BF16),
                   jax.ShapeDtypeStruct((nb, w, s_len), F32), jax.ShapeDtypeStruct((w, t), BF16)],
        compiler_params=_params("parallel"),
        name="rope",
    )(proj, proj, proj, *tables)


def _diff_lambda(lam_ref, lam_init):
    lv = lam_ref[...]
    a = jnp.sum(lv[0:1] * lv[1:2], axis=-1, keepdims=True)
    b = jnp.sum(lv[2:3] * lv[3:4], axis=-1, keepdims=True)
    return jnp.exp(a) - jnp.exp(b) + lam_init


def _subln_gate(o, sub_g, g, lam_init):
    r = lax.rsqrt(jnp.mean(o * o, axis=-1, keepdims=True) + EPS)
    return ((o * r) * sub_g) * (1.0 - lam_init) * _silu(g)


def _sublane_allreduce(x, op):
    for sh in (1, 2, 4):
        x = op(x, pltpu.roll(x, sh, 0))
    return x


def _diff_attn_body(qi_ref, ki_ref, qt_ref, k_ref, vt_ref, g_ref, lam_ref, sg_ref, o_ref,
                    m_ref, l_ref, acc_ref, *, t, lam_init):
    pidx = pl.program_id(1)
    qi = qi_ref[pidx]
    ki = ki_ref[pidx]
    ng = t // SUBLANES

    @pl.when(ki == 0)
    def _():
        m_ref[...] = jnp.full(m_ref.shape, NEG_BIG, F32)
        l_ref[...] = jnp.zeros(l_ref.shape, F32)
        acc_ref[...] = jnp.zeros(acc_ref.shape, F32)

    def head(h, masked):
        r0 = pl.multiple_of(h * LANES, LANES)
        qt = qt_ref[pl.ds(r0, LANES), :]
        k = k_ref[:, pl.ds(r0, LANES)]
        vt = vt_ref[pl.ds(r0, LANES), :]
        vt1 = jnp.concatenate([vt, jnp.ones((2 * SUBLANES, t), BF16)], axis=0)
        sub = lax.broadcasted_iota(jnp.int32, qt.shape, 0)
        zero = jnp.zeros_like(qt)
        s2 = [_dot(k, jnp.where(sub < DIFF_DK, qt, zero)), _dot(k, jnp.where(sub >= DIFF_DK, qt, zero))]
        for c in range(2):
            s = s2[c]
            if masked:
                kpos = lax.broadcasted_iota(jnp.int32, s.shape, 0)
                qpos = lax.broadcasted_iota(jnp.int32, s.shape, 1)
                s = jnp.where(kpos <= qpos, s, NEG_BIG)
            s3 = s.reshape(ng, SUBLANES, t)
            m_old = m_ref[h, c]
            m_new = jnp.maximum(m_old, _sublane_allreduce(jnp.max(s3, axis=0), jnp.maximum))
            alpha = jnp.exp(m_old - m_new)
            pb = jnp.exp((s3 - m_new[None]).reshape(t, t).astype(BF16))
            pv = _dot(vt1, pb)
            l_ref[h, c] = alpha * l_ref[h, c] + pv[DIFF_DV:DIFF_DV + SUBLANES]
            acc = acc_ref[h, c].reshape(DIFF_DV // SUBLANES, SUBLANES, t) * alpha[None]
            acc_ref[h, c] = acc.reshape(DIFF_DV, t) + pv[:DIFF_DV]
            m_ref[h, c] = m_new

    def finish(h):
        r0 = pl.multiple_of(h * LANES, LANES)
        lam = _diff_lambda(lam_ref, lam_init)
        parts = []
        for c in range(2):
            parts.append(acc_ref[h, c].reshape(DIFF_DV // SUBLANES, SUBLANES, t) / l_ref[h, c][None])
        o = (parts[0] - lam * parts[1]).reshape(DIFF_DV, t).T
        o_ref[:, pl.ds(r0, LANES)] = _subln_gate(o, sg_ref[...], g_ref[:, pl.ds(r0, LANES)], lam_init).astype(BF16)

    @pl.when(ki < qi)
    def _():
        def body(h, carry):
            head(h, False)
            return carry
        lax.fori_loop(0, DIFF_HEADS, body, 0, unroll=2)

    @pl.when(ki == qi)
    def _():
        def body(h, carry):
            head(h, True)
            finish(h)
            return carry
        lax.fori_loop(0, DIFF_HEADS, body, 0, unroll=2)


def _diff_attn_prompt(qt, kb, vt, proj, lam_vecs, sub_g, lam_init, nb, s_len, t):
    tt = kb.shape[0]
    nq = s_len // t
    pairs = [(i, j) for i in range(nq) for j in range(i + 1)]
    qi_tab = jnp.asarray([a for a, _ in pairs], jnp.int32)
    ki_tab = jnp.asarray([b for _, b in pairs], jnp.int32)
    g_cb = (SSD_INNER + 2 * DIFF_QK + DIFF_WIDTH) // DIFF_WIDTH
    w = DIFF_WIDTH
    grid_spec = pltpu.PrefetchScalarGridSpec(
        num_scalar_prefetch=2,
        grid=(nb, len(pairs)),
        in_specs=[pl.BlockSpec((w, t), lambda b, p, qtab, ktab: (0, b * nq + qtab[p])),
                  pl.BlockSpec((t, w), lambda b, p, qtab, ktab: (b * nq + ktab[p], 0)),
                  pl.BlockSpec((w, t), lambda b, p, qtab, ktab: (0, b * nq + ktab[p])),
                  pl.BlockSpec((t, w), lambda b, p, qtab, ktab: (b * nq + qtab[p], g_cb)),
                  pl.BlockSpec((4, DIFF_DK), lambda b, p, qtab, ktab: (0, 0)),
                  pl.BlockSpec((1, DIFF_DV), lambda b, p, qtab, ktab: (0, 0))],
        out_specs=pl.BlockSpec((t, w), lambda b, p, qtab, ktab: (b * nq + qtab[p], 0)),
        scratch_shapes=[pltpu.VMEM((DIFF_HEADS, 2, SUBLANES, t), F32), pltpu.VMEM((DIFF_HEADS, 2, SUBLANES, t), F32),
                        pltpu.VMEM((DIFF_HEADS, 2, DIFF_DV, t), F32)],
    )
    return pl.pallas_call(
        functools.partial(_diff_attn_body, t=t, lam_init=lam_init),
        grid_spec=grid_spec,
        out_shape=jax.ShapeDtypeStruct((tt, w), BF16),
        compiler_params=_params("parallel", "arbitrary"),
        name="diff_attn_prompt",
    )(qi_tab, ki_tab, qt, kb, vt, proj, lam_vecs, sub_g.reshape(1, DIFF_DV))


PAGES_PER_STEP = 8


def _diff_attn_sample_body(pt_ref, qt_ref, knt_ref, vn_ref, g_ref, *rest, lam_init):
    npg = PAGES_PER_STEP
    kt_refs = rest[:npg]
    v_refs = rest[npg:2 * npg]
    lam_ref, sg_ref, o_ref, qcol_ref, m_ref, l_ref, acc_ref = rest[2 * npg:]
    b = pl.program_id(0)
    p = pl.program_id(1)
    nreq = qt_ref.shape[1]
    shape4 = (DIFF_HEADS, 2, DIFF_DK, PAGE_SIZE)

    @pl.when(p == 0)
    def _():
        onehot = (lax.broadcasted_iota(jnp.int32, (nreq, LANES), 0) == b).astype(BF16)
        qcol = _dot(qt_ref[...], onehot)
        qcol_ref[...] = qcol
        kn_hi, kn_lo = _split2(knt_ref[...])
        kcol = _dot(kn_hi, onehot) + _dot(kn_lo, onehot)
        prod = (qcol * kcol).reshape(shape4)
        for c in range(2):
            m_ref[c] = jnp.sum(prod[:, c], axis=1)
            l_ref[c] = jnp.ones((DIFF_HEADS, LANES), F32)
            acc_ref[c] = vn_ref[...]

    lane = lax.broadcasted_iota(jnp.int32, (DIFF_HEADS, LANES), 1)
    sub_w = lax.broadcasted_iota(jnp.int32, (DIFF_HEADS, PAGE_SIZE * DIFF_HEADS), 0)
    lane_w = lax.broadcasted_iota(jnp.int32, (DIFF_HEADS, PAGE_SIZE * DIFF_HEADS), 1)
    own_head = sub_w == lane_w % DIFF_HEADS
    keys_per_block = LANES // DIFF_HEADS
    for i in range(npg):
        prod = (qcol_ref[...] * kt_refs[i][...]).reshape(shape4)
        v2 = v_refs[i][...].reshape(PAGE_SIZE * DIFF_HEADS, DIFF_DV).astype(BF16)
        ws, alphas = [], []
        for c in range(2):
            s = jnp.sum(prod[:, c], axis=1)
            m_old = m_ref[c]
            m_new = jnp.maximum(m_old, jnp.max(s, axis=-1, keepdims=True))
            alpha = jnp.exp(m_old - m_new)
            pr = jnp.exp(s - m_new)
            l_ref[c] = alpha * l_ref[c] + jnp.sum(pr, axis=-1, keepdims=True)
            m_ref[c] = m_new
            wexp = jnp.concatenate(
                [jnp.take_along_axis(pr, keys_per_block * j + lane // DIFF_HEADS, axis=1)
                 for j in range(DIFF_HEADS)], axis=1)
            ws.append(jnp.where(own_head, wexp, 0.0))
            alphas.append(alpha)
        pv = _dot(jnp.concatenate(ws, axis=0).astype(BF16), v2)
        for c in range(2):
            acc_ref[c] = alphas[c] * acc_ref[c] + pv[c * DIFF_HEADS:(c + 1) * DIFF_HEADS]

    @pl.when(p == pl.num_programs(1) - 1)
    def _():
        lam = _diff_lambda(lam_ref, lam_init)
        o = acc_ref[0] / l_ref[0] - lam * (acc_ref[1] / l_ref[1])
        o_ref[...] = _subln_gate(o, sg_ref[...], g_ref[...], lam_init).astype(BF16)


def _diff_attn_sample(qt, kn_t, proj, cache_kt, cache_v, n_pool, page_table, layer_a, lam_vecs, sub_g, lam_init):
    nreq = kn_t.shape[1]
    n_pages = page_table.shape[1]
    npg = PAGES_PER_STEP
    assert n_pages % npg == 0
    base = layer_a * n_pool
    proj3 = proj.reshape(nreq, proj.shape[1] // DIFF_DV, DIFF_DV)
    v_cb = (SSD_INNER + 2 * DIFF_QK) // DIFF_WIDTH
    head_rows = lambda cb: pl.BlockSpec((None, DIFF_HEADS, DIFF_DV), lambda b, p, pt: (b, cb, 0))
    const = lambda shape: pl.BlockSpec(shape, lambda b, p, pt: (0,) * len(shape))

    def page_spec(shape, i):
        zeros = (0,) * (len(shape) - 1)
        return pl.BlockSpec(shape, lambda b, p, pt: (base + pt[b * n_pages + p * npg + i],) + zeros)

    grid_spec = pltpu.PrefetchScalarGridSpec(
        num_scalar_prefetch=1,
        grid=(nreq, n_pages // npg),
        in_specs=([const((DIFF_QK, nreq)), const((DIFF_QK, nreq)), head_rows(v_cb), head_rows(v_cb + 1)]
                  + [page_spec((None, DIFF_QK, PAGE_SIZE), i) for i in range(npg)]
                  + [page_spec((None, PAGE_SIZE, DIFF_HEADS, DIFF_DV), i) for i in range(npg)]
                  + [const((4, DIFF_DK)), const((1, DIFF_DV))]),
        out_specs=head_rows(0),
        scratch_shapes=[pltpu.VMEM((DIFF_QK, LANES), F32), pltpu.VMEM((2, DIFF_HEADS, LANES), F32),
                        pltpu.VMEM((2, DIFF_HEADS, LANES), F32), pltpu.VMEM((2, DIFF_HEADS, DIFF_DV), F32)],
    )
    out = pl.pallas_call(
        functools.partial(_diff_attn_sample_body, lam_init=lam_init),
        grid_spec=grid_spec,
        out_shape=jax.ShapeDtypeStruct((nreq, DIFF_HEADS, DIFF_DV), BF16),
        compiler_params=_params("parallel", "arbitrary"),
        name="diff_attn_sample",
    )(page_table.reshape(-1), qt, kn_t, proj3, proj3, *([cache_kt] * npg), *([cache_v] * npg), lam_vecs,
      sub_g.reshape(1, DIFF_DV))
    return out.reshape(nreq, DIFF_WIDTH)


def _xattn_heads(xq, xg, mk, mv):
    outs = []
    for h in range(X_HEADS):
        sl = slice(h * X_HEADDIM, (h + 1) * X_HEADDIM)
        s = _dot_nt(xq[:, sl].astype(BF16), mk[:, sl].astype(BF16)) * (X_HEADDIM ** -0.5)
        e = jnp.exp(s - jnp.max(s, axis=-1, keepdims=True))
        pr = (e / jnp.sum(e, axis=-1, keepdims=True)).astype(BF16)
        outs.append(_dot(pr, mv[:, sl].astype(BF16)) * _silu(xg[:, sl]))
    return outs


def _xattn_prompt_body(xq_ref, xg_ref, mk_ref, mv_ref, o_ref):
    outs = _xattn_heads(xq_ref[...], xg_ref[...], mk_ref[...], mv_ref[...])
    for h in range(X_HEADS):
        o_ref[:, h * X_HEADDIM:(h + 1) * X_HEADDIM] = outs[h].astype(BF16)


def _xattn_prompt(proj, xq_cb, mem_kv, layer, nb, s_len, tq):
    t = proj.shape[0]
    nq = s_len // tq
    return pl.pallas_call(
        _xattn_prompt_body,
        grid=(nb, nq),
        in_specs=[pl.BlockSpec((tq, X_WIDTH), lambda b, i: (b * nq + i, xq_cb)),
                  pl.BlockSpec((tq, X_WIDTH), lambda b, i: (b * nq + i, xq_cb + 1)),
                  pl.BlockSpec((None, N_MEM, X_WIDTH), lambda b, i: (b, 0, 2 * layer)),
                  pl.BlockSpec((None, N_MEM, X_WIDTH), lambda b, i: (b, 0, 2 * layer + 1))],
        out_specs=pl.BlockSpec((tq, X_WIDTH), lambda b, i: (b * nq + i, 0)),
        out_shape=jax.ShapeDtypeStruct((t, X_WIDTH), BF16),
        compiler_params=_params("parallel", "parallel"),
        name="xattn_prompt",
    )(proj, proj, mem_kv, mem_kv)


def _xattn_sample_body(xqg_ref, mk_ref, mv_ref, o_ref, *, rb):
    shape = (2 * X_HEADS, N_MEM * X_HEADS)
    own_head = lax.broadcasted_iota(jnp.int32, shape, 0) == lax.broadcasted_iota(jnp.int32, shape, 1) % X_HEADS
    scores = []
    for r in range(rb):
        s = _dot_nt(xqg_ref[r].astype(BF16), mk_ref[r].astype(BF16)) * (X_HEADDIM ** -0.5)
        scores.append(jnp.where(own_head, s, NEG_BIG))
    probs = []
    for s in scores:
        e = jnp.exp(s - jnp.max(s, axis=-1, keepdims=True))
        probs.append((e / jnp.sum(e, axis=-1, keepdims=True)).astype(BF16))
    for r in range(rb):
        o = _dot(probs[r], mv_ref[r].astype(BF16))[0:X_HEADS]
        o_ref[r] = (o * _silu(xqg_ref[r, X_HEADS:2 * X_HEADS, :])).astype(BF16)


def _xattn_sample(proj, xq_cb, mem_k, mem_v, layer, rb):
    nreq = proj.shape[0]
    p3 = proj.reshape(nreq, proj.shape[1] // X_HEADDIM, X_HEADDIM)
    mem_spec = pl.BlockSpec((None, rb, N_MEM * X_HEADS, X_HEADDIM), lambda i: (layer, i, 0, 0))
    out = pl.pallas_call(
        functools.partial(_xattn_sample_body, rb=rb),
        grid=(nreq // rb,),
        in_specs=[pl.BlockSpec((rb, 2 * X_HEADS, X_HEADDIM), lambda i: (i, xq_cb // 2, 0)), mem_spec, mem_spec],
        out_specs=pl.BlockSpec((rb, X_HEADS, X_HEADDIM), lambda i: (i, 0, 0)),
        out_shape=jax.ShapeDtypeStruct((nreq, X_HEADS, X_HEADDIM), BF16),
        compiler_params=_params("parallel"),
        name="xattn_sample",
    )(p3, mem_k, mem_v)
    return out.reshape(nreq, X_WIDTH)


def _ssd_consts(dt_bias, a_log, d_skip):
    pad = LANES - SSD_HEADS
    dtb = jnp.pad(dt_bias.astype(F32), (0, pad)).reshape(1, LANES)
    alog = jnp.pad(a_log.astype(F32), (0, pad)).reshape(1, LANES)
    dexp = jnp.repeat(d_skip.astype(F32), SSD_HEADDIM).reshape(1, SSD_INNER)
    head = np.arange(LANES)[:, None]
    chan_head = np.arange(SSD_INNER)[None, :] // SSD_HEADDIM
    eexp = jnp.asarray((head == chan_head).astype(np.float32), BF16)
    return dtb, alog, dexp, eexp


def _softplus(x):
    return jnp.maximum(x, 0.0) + jnp.log1p(jnp.exp(-jnp.abs(x)))


def _ssd_gate_norm(y, xs, z, dexp, ng):
    y = (y + dexp * xs) * _silu(z)
    r = lax.rsqrt(jnp.mean(y * y, axis=-1, keepdims=True) + EPS)
    return ((y * r) * ng).astype(BF16)


def _ssd_prompt_body(z_ref, xbc_ref, dtr_ref, cw_ref, cb_ref, dtb_ref, alog_ref, dexp_ref, ng_ref, eexp_ref,
                     y_ref, hout_ref, ext_ref, h_ref, yd_ref):
    q = SSD_CHUNK
    c = pl.program_id(1)

    @pl.when(c == 0)
    def _():
        ext_ref[0:SUBLANES, :] = jnp.zeros((SUBLANES, SSD_CONV_CH), F32)
        h_ref[...] = jnp.zeros(h_ref.shape, F32)

    x_raw = xbc_ref[...]
    ext_ref[SUBLANES:SUBLANES + q, :] = x_raw
    conv = cb_ref[...] + cw_ref[SSD_CONV - 1:SSD_CONV, :] * x_raw
    for k in range(1, SSD_CONV):
        conv = conv + cw_ref[SSD_CONV - 1 - k:SSD_CONV - k, :] * ext_ref[SUBLANES - k:SUBLANES - k + q, :]
    ext_ref[0:SUBLANES, :] = x_raw[q - SUBLANES:q, :]
    xbc = _silu(conv)
    xs = xbc[:, :SSD_INNER]
    bm = xbc[:, SSD_INNER:SSD_INNER + SSD_GROUPS * SSD_STATE].astype(BF16)
    cm = xbc[:, SSD_INNER + SSD_GROUPS * SSD_STATE:].astype(BF16)

    eexp = eexp_ref[...]
    dt = _softplus(dtr_ref[...] + dtb_ref[...])
    da = dt * (-jnp.exp(alog_ref[...]))
    ri = lax.broadcasted_iota(jnp.int32, (q, q), 0)
    ci = lax.broadcasted_iota(jnp.int32, (q, q), 1)
    tri = ri >= ci
    tri_b = tri.astype(BF16)
    hi, mid, lo = _split3(da)
    acum = _dot(tri_b, hi) + _dot(tri_b, mid) + _dot(tri_b, lo)
    acum_t = acum.T
    a_last = acum[q - 1:q, :]
    xdt = xs * _expand(_split2(dt), eexp)
    xdt_b = xdt.astype(BF16)
    ea_exp = _expand(_split2(jnp.exp(acum)), eexp)
    wend_exp = _expand(_split2(jnp.exp(a_last - acum)), eexp)
    xw_t = (xdt * wend_exp).T.astype(BF16)
    decay = jnp.exp(a_last)

    lane = lax.broadcasted_iota(jnp.int32, (q, LANES), 1)
    for g in range(SSD_GROUPS):
        gs = slice(g * SSD_STATE, (g + 1) * SSD_STATE)
        ch = slice(g * SSD_HPG * SSD_HEADDIM, (g + 1) * SSD_HPG * SSD_HEADDIM)
        scores = _dot_nt(cm[:, gs], bm[:, gs])
        for pair in range(SSD_HPG // 2):
            halves = []
            for r in range(2):
                hd = g * SSD_HPG + 2 * pair + r
                seg = acum[:, hd:hd + 1] - acum_t[hd:hd + 1, :]
                ldec = jnp.where(tri, jnp.exp(jnp.where(tri, seg, 0.0)), 0.0)
                xp = xdt_b[:, (hd // 2) * LANES:(hd // 2 + 1) * LANES]
                halves.append(_dot((scores * ldec).astype(BF16), xp))
            lo_lane = (g * SSD_HPG + 2 * pair) // 2 * LANES
            yd_ref[:, lo_lane:lo_lane + LANES] = jnp.where(lane < SSD_HEADDIM, halves[0], halves[1])
        h_g = h_ref[ch, :]
        yd_ref[:, ch] = yd_ref[:, ch] + _dot_nt(cm[:, gs], h_g.astype(BF16)) * ea_exp[:, ch]
        upd = _dot(xw_t[ch, :], bm[:, gs])
        for r in range(SSD_HPG):
            hd = g * SSD_HPG + r
            rows = slice(hd * SSD_HEADDIM, (hd + 1) * SSD_HEADDIM)
            h_ref[rows, :] = h_ref[rows, :] * decay[:, hd:hd + 1] + upd[r * SSD_HEADDIM:(r + 1) * SSD_HEADDIM, :]

    y_ref[...] = _ssd_gate_norm(yd_ref[...], xs, z_ref[...], dexp_ref[...], ng_ref[...])
    hout_ref[...] = h_ref[...]


def _ssd_prompt(proj, proj_s, conv_w, conv_b, consts, norm_g, nb, s_len):
    t = proj.shape[0]
    q = SSD_CHUNK
    nc = s_len // q
    dtb, alog, dexp, eexp = consts
    const = lambda shape: pl.BlockSpec(shape, lambda b, c: (0,) * len(shape))
    return pl.pallas_call(
        _ssd_prompt_body,
        grid=(nb, nc),
        in_specs=[pl.BlockSpec((q, SSD_INNER), lambda b, c: (b * nc + c, 0)),
                  pl.BlockSpec((q, SSD_CONV_CH), lambda b, c: (b * nc + c, 0)),
                  pl.BlockSpec((q, LANES), lambda b, c: (b * nc + c, SSD_CONV_CH // LANES)),
                  const((SSD_CONV, SSD_CONV_CH)), const((1, SSD_CONV_CH)), const((1, LANES)), const((1, LANES)),
                  const((1, SSD_INNER)), const((1, SSD_INNER)), const((LANES, SSD_INNER))],
        out_specs=[pl.BlockSpec((q, SSD_INNER), lambda b, c: (b * nc + c, 0)),
                   pl.BlockSpec((None, SSD_INNER, SSD_STATE), lambda b, c: (b, 0, 0))],
        out_shape=[jax.ShapeDtypeStruct((t, SSD_INNER), BF16),
                   jax.ShapeDtypeStruct((nb, SSD_INNER, SSD_STATE), F32)],
        scratch_shapes=[pltpu.VMEM((SUBLANES + q, SSD_CONV_CH), F32), pltpu.VMEM((SSD_INNER, SSD_STATE), F32),
                        pltpu.VMEM((q, SSD_INNER), F32)],
        compiler_params=_params("parallel", "arbitrary"),
        name="ssd_prompt",
    )(proj, proj_s, proj_s, conv_w, conv_b.reshape(1, -1), dtb, alog, dexp, norm_g.reshape(1, -1), eexp)


def _ssd_sample_body(z_ref, xbc_ref, dtr_ref, cs_ref, hin_ref, cw_ref, cb_ref, dtb_ref, alog_ref, dexp_ref, ng_ref,
                     eexp_ref, y_ref, hout_ref, xs_ref, xdt_t_ref, da_hi_ref, da_lo_ref, bm_ref, cm_t_ref, y_t_ref,
                     *, rb):
    i = pl.program_id(0)
    nreq = z_ref.shape[0]

    @pl.when(i == 0)
    def _():
        conv = cb_ref[...] + cw_ref[SSD_CONV - 1:SSD_CONV, :] * xbc_ref[...]
        for k in range(SSD_CONV - 1):
            conv = conv + cw_ref[k:k + 1, :] * cs_ref[k]
        xbc = _silu(conv)
        xs = xbc[:, :SSD_INNER]
        xs_ref[...] = xs
        bm_ref[...] = xbc[:, SSD_INNER:SSD_INNER + SSD_GROUPS * SSD_STATE]
        cm_t_ref[...] = xbc[:, SSD_INNER + SSD_GROUPS * SSD_STATE:].T
        dt = _softplus(dtr_ref[...] + dtb_ref[...])
        da_hi, da_lo = _split2(jnp.exp(dt * (-jnp.exp(alog_ref[...]))).T)
        da_hi_ref[...] = da_hi
        da_lo_ref[...] = da_lo
        xdt_t_ref[...] = (xs * _expand(_split2(dt), eexp_ref[...])).T.astype(BF16)
        y_t_ref[...] = jnp.zeros(y_t_ref.shape, F32)

    def per_request(r, carry):
        b = i * rb + r
        onehot = (lax.broadcasted_iota(jnp.int32, (nreq, LANES), 0) == b).astype(BF16)
        xb = _dot(xdt_t_ref[...], onehot)
        dab = _dot(da_hi_ref[...], onehot) + _dot(da_lo_ref[...], onehot)
        brow = bm_ref[pl.ds(b, 1), :]
        cmask = lax.broadcasted_iota(jnp.int32, (SSD_STATE, nreq), 1) == b
        for g in range(SSD_GROUPS):
            gs = slice(g * SSD_STATE, (g + 1) * SSD_STATE)
            ch = slice(g * SSD_HPG * SSD_HEADDIM, (g + 1) * SSD_HPG * SSD_HEADDIM)
            for hr in range(SSD_HPG):
                hd = g * SSD_HPG + hr
                rows = slice(hd * SSD_HEADDIM, (hd + 1) * SSD_HEADDIM)
                hout_ref[r, rows, :] = dab[hd:hd + 1, :] * hin_ref[r, rows, :] + xb[rows, :] * brow[:, gs]
            cmat = jnp.where(cmask, cm_t_ref[gs, :], 0.0).astype(BF16)
            y_t_ref[ch, :] = y_t_ref[ch, :] + _dot(hout_ref[r, ch, :].astype(BF16), cmat)
        return carry

    lax.fori_loop(0, rb, per_request, 0)

    @pl.when(i == pl.num_programs(0) - 1)
    def _():
        y_ref[...] = _ssd_gate_norm(y_t_ref[...].T, xs_ref[...], z_ref[...], dexp_ref[...], ng_ref[...])


def _ssd_sample(proj, proj_s, conv_state_t, h_state, layer_a, conv_w, conv_b, consts, norm_g, rb):
    nreq = proj.shape[0]
    dtb, alog, dexp, eexp = consts
    const = lambda shape: pl.BlockSpec(shape, lambda i: (0,) * len(shape))
    return pl.pallas_call(
        functools.partial(_ssd_sample_body, rb=rb),
        grid=(nreq // rb,),
        in_specs=[pl.BlockSpec((nreq, SSD_INNER), lambda i: (0, 0)),
                  pl.BlockSpec((nreq, SSD_CONV_CH), lambda i: (0, 0)),
                  pl.BlockSpec((nreq, LANES), lambda i: (0, SSD_CONV_CH // LANES)),
                  const((SSD_CONV - 1, nreq, SSD_CONV_CH)),
                  pl.BlockSpec((None, rb, SSD_INNER, SSD_STATE), lambda i: (layer_a, i, 0, 0)),
                  const((SSD_CONV, SSD_CONV_CH)), const((1, SSD_CONV_CH)), const((1, LANES)), const((1, LANES)),
                  const((1, SSD_INNER)), const((1, SSD_INNER)), const((LANES, SSD_INNER))],
        out_specs=[const((nreq, SSD_INNER)),
                   pl.BlockSpec((rb, SSD_INNER, SSD_STATE), lambda i: (i, 0, 0))],
        out_shape=[jax.ShapeDtypeStruct((nreq, SSD_INNER), BF16),
                   jax.ShapeDtypeStruct(h_state.shape[1:], F32)],
        scratch_shapes=[pltpu.VMEM((nreq, SSD_INNER), F32), pltpu.VMEM((SSD_INNER, nreq), BF16),
                        pltpu.VMEM((LANES, nreq), BF16), pltpu.VMEM((LANES, nreq), BF16),
                        pltpu.VMEM((nreq, SSD_GROUPS * SSD_STATE), F32),
                        pltpu.VMEM((SSD_GROUPS * SSD_STATE, nreq), F32), pltpu.VMEM((SSD_INNER, nreq), F32)],
        compiler_params=_params("arbitrary"),
        name="ssd_sample",
    )(proj, proj_s, proj_s, conv_state_t, h_state, conv_w, conv_b.reshape(1, -1), dtb, alog, dexp,
      norm_g.reshape(1, -1), eexp)


CONF_PAD = 32
CONF_RB = 64
CONF_CB = LANES
CONF_NCB = CONF_WIDTH // CONF_CB


def _conf_finish(c, g, ln_g, ln_b):
    mu = jnp.mean(c, axis=-1, keepdims=True)
    xc = c - mu
    r = lax.rsqrt(jnp.mean(xc * xc, axis=-1, keepdims=True) + EPS)
    return (_silu((xc * r) * ln_g + ln_b) * _silu(g)).astype(BF16)


def _conf_prompt_body(a_ref, ag_ref, g_ref, w_ref, b_ref, lg_ref, lb_ref, y_ref, tail_ref, sh_ref, c_ref, *, tt):
    i = pl.program_id(1)
    rows = CONF_PAD + tt

    @pl.when(i == 0)
    def _():
        sh_ref[0, :, 0:CONF_PAD, :] = jnp.zeros((CONF_NCB, CONF_PAD, CONF_CB), F32)

    a = a_ref[...]
    ag = ag_ref[...]
    u = a / (1.0 + jnp.exp(-ag))
    tail_ref[...] = u[tt - CONF_PAD:tt, :]
    for cb in range(CONF_NCB):
        sh_ref[0, cb, CONF_PAD:rows, :] = u[:, cb * CONF_CB:(cb + 1) * CONF_CB]
        for s in range(1, SUBLANES):
            sh_ref[s, cb, 0:rows - SUBLANES, :] = sh_ref[0, cb, s:s + rows - SUBLANES, :]

    first = CONF_PAD - (CONF_CONV_W - 1)

    def tile(idx, carry):
        rb = idx // CONF_NCB
        cb = idx % CONF_NCB
        r0 = pl.multiple_of(rb * CONF_RB, CONF_RB)
        acc = jnp.broadcast_to(b_ref[cb], (CONF_RB, CONF_CB))
        for k in range(CONF_CONV_W):
            off = first + k
            src = sh_ref[off % SUBLANES, cb, pl.ds(r0 + (off // SUBLANES) * SUBLANES, CONF_RB), :]
            acc = acc + w_ref[k, cb] * src
        c_ref[cb, pl.ds(r0, CONF_RB), :] = acc
        return carry

    lax.fori_loop(0, (tt // CONF_RB) * CONF_NCB, tile, 0)
    for cb in range(CONF_NCB):
        sh_ref[0, cb, 0:CONF_PAD, :] = sh_ref[0, cb, tt:rows, :]
    c = jnp.concatenate([c_ref[cb] for cb in range(CONF_NCB)], axis=1)
    y_ref[...] = _conf_finish(c, g_ref[...], lg_ref[...], lb_ref[...])


def _conf_prompt(proj, dw_w, dw_b, ln_g, ln_b, nb, s_len, tt):
    t = proj.shape[0]
    nt = s_len // tt
    col = lambda cb: pl.BlockSpec((tt, CONF_WIDTH), lambda b, i: (b * nt + i, cb))
    const = lambda shape: pl.BlockSpec(shape, lambda b, i: (0,) * len(shape))
    return pl.pallas_call(
        functools.partial(_conf_prompt_body, tt=tt),
        grid=(nb, nt),
        in_specs=[col(0), col(1), col(2), const((CONF_CONV_W, CONF_NCB, 1, CONF_CB)), const((CONF_NCB, 1, CONF_CB)),
                  const((1, CONF_WIDTH)), const((1, CONF_WIDTH))],
        out_specs=[pl.BlockSpec((tt, CONF_WIDTH), lambda b, i: (b * nt + i, 0)),
                   pl.BlockSpec((None, CONF_PAD, CONF_WIDTH), lambda b, i: (b, 0, 0))],
        out_shape=[jax.ShapeDtypeStruct((t, CONF_WIDTH), BF16),
                   jax.ShapeDtypeStruct((nb, CONF_PAD, CONF_WIDTH), F32)],
        scratch_shapes=[pltpu.VMEM((SUBLANES, CONF_NCB, CONF_PAD + tt, CONF_CB), F32),
                        pltpu.VMEM((CONF_NCB, tt, CONF_CB), F32)],
        compiler_params=_params("parallel", "arbitrary"),
        name="conf_prompt",
    )(proj, proj, proj, dw_w.reshape(CONF_CONV_W, CONF_NCB, 1, CONF_CB), dw_b.reshape(CONF_NCB, 1, CONF_CB),
      ln_g.reshape(1, -1), ln_b.reshape(1, -1))


def _conf_sample_body(a_ref, ag_ref, g_ref, st_ref, w_ref, b_ref, lg_ref, lb_ref, y_ref, u_ref, c_ref, *, tc):
    j = pl.program_id(0)
    u = a_ref[...] / (1.0 + jnp.exp(-ag_ref[...]))
    u_ref[...] = u
    acc = b_ref[...] + w_ref[CONF_CONV_W - 1:CONF_CONV_W, :] * u
    for k in range(CONF_CONV_W - 1):
        acc = acc + w_ref[k:k + 1, :] * st_ref[k]
    c_ref[j] = acc

    @pl.when(j == pl.num_programs(0) - 1)
    def _():
        c = jnp.concatenate([c_ref[k] for k in range(CONF_WIDTH // tc)], axis=1)
        y_ref[...] = _conf_finish(c, g_ref[...], lg_ref[...], lb_ref[...])


def _conf_sample(proj, state_t, dw_w, dw_b, ln_g, ln_b, tc):
    nreq = proj.shape[0]
    nblk = CONF_WIDTH // tc
    col = lambda off: pl.BlockSpec((nreq, tc), lambda j: (0, off * nblk + j))
    vec = pl.BlockSpec((1, tc), lambda j: (0, j))
    full = pl.BlockSpec((1, CONF_WIDTH), lambda j: (0, 0))
    return pl.pallas_call(
        functools.partial(_conf_sample_body, tc=tc),
        grid=(nblk,),
        in_specs=[col(0), col(1), pl.BlockSpec((nreq, CONF_WIDTH), lambda j: (0, 2)),
                  pl.BlockSpec((CONF_CONV_W - 1, nreq, tc), lambda j: (0, 0, j)),
                  pl.BlockSpec((CONF_CONV_W, tc), lambda j: (0, j)), vec, full, full],
        out_specs=[pl.BlockSpec((nreq, CONF_WIDTH), lambda j: (0, 0)),
                   pl.BlockSpec((nreq, tc), lambda j: (0, j))],
        out_shape=[jax.ShapeDtypeStruct((nreq, CONF_WIDTH), BF16),
                   jax.ShapeDtypeStruct((nreq, CONF_WIDTH), F32)],
        scratch_shapes=[pltpu.VMEM((nblk, nreq, tc), F32)],
        compiler_params=_params("arbitrary"),
        name="conf_sample",
    )(proj, proj, proj, state_t, dw_w, dw_b.reshape(1, -1), ln_g.reshape(1, -1), ln_b.reshape(1, -1))


def _even_weights(w_in):
    cuts = np.cumsum([SSD_INNER, SSD_CONV_CH, SSD_HEADS]).tolist()
    z, xbc, dt, rest = (w_in[:, :cuts[0]], w_in[:, cuts[0]:cuts[1]], w_in[:, cuts[1]:cuts[2]], w_in[:, cuts[2]:])
    main = jnp.concatenate([z, rest], axis=1).astype(BF16)
    pad = jnp.zeros((w_in.shape[0], LANES - SSD_HEADS), w_in.dtype)
    ssd = jnp.concatenate([xbc, dt, pad], axis=1).astype(BF16)
    return main, ssd


def kernel(x_prompt, x_sample, cache_attn_k, cache_attn_v, cache_mem_k, cache_mem_v, state_ssm, state_ssm_conv,
           state_conf_conv, page_table, mem_prompt, norm_a, w_in_a, ssm_conv_w, ssm_conv_b, ssm_dt_bias, ssm_a_log,
           ssm_d, ssm_norm, diff_lambda, diff_subln, w_out_a, norm_c, w_in_c, conf_dw_w, conf_dw_b, conf_ln_g,
           conf_ln_b, w_out_c, xattn_wk, xattn_wv, mem_norm, norm_f):
    bp, sp, d = x_prompt.shape
    bs = x_sample.shape[0]
    n_a = w_in_a.shape[0]
    n_pool = cache_attn_k.shape[1]
    past = page_table.shape[1] * PAGE_SIZE
    tp = bp * sp

    tm_p = min(1024, sp)
    t_attn = min(512, sp)
    tm_rope = min(512, sp)
    rb_x = 8
    tq_x = min(512, sp)
    tt_conf = min(256, sp)
    tm_out = min(512, sp)

    xp = x_prompt.reshape(tp, d)
    xs = x_sample.reshape(bs, d)

    w_mem = jnp.concatenate([jnp.concatenate([xattn_wk[l], xattn_wv[l]], axis=1) for l in range(DEPTH)],
                            axis=1).astype(BF16)
    mem_kv = _norm_matmul(mem_prompt.reshape(bp * N_MEM, d), mem_norm, w_mem, N_MEM, 1024)
    mem_kv3 = mem_kv.reshape(bp, N_MEM, DEPTH * 2 * X_WIDTH)
    mem_kv5 = mem_kv.reshape(bp, N_MEM, DEPTH, 2, X_HEADS, X_HEADDIM)
    p_mem_k = jnp.moveaxis(mem_kv5[:, :, :, 0], 2, 0)
    p_mem_v = jnp.moveaxis(mem_kv5[:, :, :, 1], 2, 0)

    tab_p = _rope_tables(jnp.arange(sp))
    tab_s = _rope_tables(jnp.full((bs,), past, jnp.int32))
    cache_kt = jnp.swapaxes(cache_attn_k.reshape(n_a * n_pool, PAGE_SIZE, DIFF_QK), 1, 2)
    cache_v4 = cache_attn_v.reshape(n_a * n_pool, PAGE_SIZE, DIFF_HEADS, DIFF_DV)
    h_state = state_ssm.reshape(n_a, bs, SSD_INNER, SSD_STATE)
    mem_k_rows = cache_mem_k.reshape(DEPTH, bs, N_MEM * X_HEADS, X_HEADDIM)
    mem_v_rows = cache_mem_v.reshape(DEPTH, bs, N_MEM * X_HEADS, X_HEADDIM)

    pk, pv, pssm, pconv, pconf = [], [], [], [], []
    sk, sv, sssm, sconv, sconf = [], [], [], [], []
    for layer in range(DEPTH):
        j = layer // 2
        final = layer == DEPTH - 1
        if layer % 2 == 0:
            lam_init = 0.8 - 0.6 * math.exp(-0.3 * layer)
            w_main, w_ssd = _even_weights(w_in_a[j])
            w_out = w_out_a[j].astype(BF16)
            consts = _ssd_consts(ssm_dt_bias[j], ssm_a_log[j], ssm_d[j])
            xq_cb = (SSD_INNER + 2 * DIFF_QK + 2 * DIFF_WIDTH) // X_WIDTH
            v_lo = SSD_INNER + 2 * DIFF_QK

            proj = _norm_matmul(xp, norm_a[j], w_main, tm_p, 2048)
            proj_s = _norm_matmul(xp, norm_a[j], w_ssd, tm_p, w_ssd.shape[1])
            y_ssd, h_last = _ssd_prompt(proj, proj_s, ssm_conv_w[j], ssm_conv_b[j], consts, ssm_norm[j], bp, sp)
            qt, kb, k_rot_t, vt = _rope(proj, tab_p, bp, sp, tm_rope)
            y_diff = _diff_attn_prompt(qt, kb, vt, proj, diff_lambda[j], diff_subln[j], lam_init, bp, sp, t_attn)
            y_x = _xattn_prompt(proj, xq_cb, mem_kv3, layer, bp, sp, tq_x)
            xp = _out_proj([y_ssd, y_diff, y_x], w_out, xp, norm_f, final, tm_out)
            pk.append(k_rot_t)
            pv.append(proj[:, v_lo:v_lo + DIFF_WIDTH].reshape(bp, sp, DIFF_HEADS, DIFF_DV))
            pssm.append(h_last.reshape(bp, SSD_GROUPS, SSD_HPG, SSD_HEADDIM, SSD_STATE))
            pconv.append(proj_s.reshape(bp, sp, -1)[:, sp - (SSD_CONV - 1):, :SSD_CONV_CH])

            proj = _norm_matmul(xs, norm_a[j], w_main, bs, 1024)
            proj_s = _norm_matmul(xs, norm_a[j], w_ssd, bs, w_ssd.shape[1])
            conv_t = jnp.swapaxes(state_ssm_conv[j], 0, 1)
            y_ssd, h_new = _ssd_sample(proj, proj_s, conv_t, h_state, j, ssm_conv_w[j], ssm_conv_b[j], consts,
                                       ssm_norm[j], 8)
            qt, _, k_rot_t, _ = _rope(proj, tab_s, 1, bs, bs)
            y_diff = _diff_attn_sample(qt, k_rot_t[0], proj, cache_kt, cache_v4, n_pool, page_table, j, diff_lambda[j],
                                       diff_subln[j], lam_init)
            y_x = _xattn_sample(proj, xq_cb, mem_k_rows, mem_v_rows, layer, rb_x)
            xs = _out_proj([y_ssd, y_diff, y_x], w_out, xs, norm_f, final, bs)
            sk.append(k_rot_t[0].T.reshape(bs, 1, DIFF_HEADS, 2, DIFF_DK))
            sv.append(proj[:, v_lo:v_lo + DIFF_WIDTH].reshape(bs, 1, DIFF_HEADS, DIFF_DV))
            sssm.append(h_new.reshape(bs, SSD_GROUPS, SSD_HPG, SSD_HEADDIM, SSD_STATE))
            sconv.append(jnp.concatenate([state_ssm_conv[j][:, 1:], proj_s[:, None, :SSD_CONV_CH]], axis=1))
        else:
            w_in = w_in_c[j].astype(BF16)
            w_out = w_out_c[j].astype(BF16)
            xq_cb = 3 * CONF_WIDTH // X_WIDTH

            proj = _norm_matmul(xp, norm_c[j], w_in, tm_p, 1024)
            y_conf, tail = _conf_prompt(proj, conf_dw_w[j], conf_dw_b[j], conf_ln_g[j], conf_ln_b[j], bp, sp, tt_conf)
            y_x = _xattn_prompt(proj, xq_cb, mem_kv3, layer, bp, sp, tq_x)
            xp = _out_proj([y_conf, y_x], w_out, xp, norm_f, final, tm_out)
            pconf.append(tail[:, CONF_PAD - (CONF_CONV_W - 1):])

            proj = _norm_matmul(xs, norm_c[j], w_in, bs, 1024)
            st_t = jnp.swapaxes(state_conf_conv[j], 0, 1)
            y_conf, u = _conf_sample(proj, st_t, conf_dw_w[j], conf_dw_b[j], conf_ln_g[j], conf_ln_b[j], 256)
            y_x = _xattn_sample(proj, xq_cb, mem_k_rows, mem_v_rows, layer, rb_x)
            xs = _out_proj([y_conf, y_x], w_out, xs, norm_f, final, bs)
            sconf.append(jnp.concatenate([state_conf_conv[j][:, 1:], u[:, None]], axis=1))

    p_attn_k = jnp.transpose(jnp.stack(pk).reshape(n_a, bp, DIFF_HEADS, 2, DIFF_DK, sp), (0, 1, 5, 2, 3, 4))
    return (xp.reshape(bp, sp, d), xs.reshape(bs, 1, d), p_attn_k, jnp.stack(pv), jnp.stack(pssm),
            jnp.stack(pconv), jnp.stack(pconf), p_mem_k, p_mem_v, jnp.stack(sk), jnp.stack(sv), jnp.stack(sssm),
            jnp.stack(sconv), jnp.stack(sconf))
```

```python
import functools
import math

import jax
import jax.numpy as jnp
import numpy as np
from jax import lax
from jax.experimental import pallas as pl
from jax.experimental.pallas import tpu as pltpu

F32 = jnp.float32
BF16 = jnp.bfloat16

D_MODEL = 1024
DEPTH = 4
PAGE_SIZE = 128
SSD_INNER = 1024
SSD_HEADDIM = 64
SSD_HEADS = 16
SSD_GROUPS = 2
SSD_HPG = 8
SSD_STATE = 128
SSD_CONV = 4
SSD_CHUNK = 128
SSD_CONV_CH = 1536
DIFF_HEADS = 8
DIFF_DK = 64
DIFF_DV = 128
DIFF_QK = 1024
DIFF_WIDTH = 1024
ROT_DIM = 16
ROPE_THETA = 500000.0
CONF_WIDTH = 2048
CONF_CONV_W = 31
N_MEM = 256
X_HEADS = 4
X_HEADDIM = 128
X_WIDTH = 512
EPS = 1e-6

LANES = 128
SUBLANES = 8
VMEM_LIMIT_BYTES = 56 * 1024 * 1024
NEG_BIG = -1e30


def _params(*sem):
    return pltpu.CompilerParams(dimension_semantics=sem, vmem_limit_bytes=VMEM_LIMIT_BYTES)


def _silu(x):
    return x / (1.0 + jnp.exp(-x))


def _split2(x):
    hi = x.astype(BF16)
    lo = (x - hi.astype(F32)).astype(BF16)
    return hi, lo


def _split3(x):
    hi = x.astype(BF16)
    r = x - hi.astype(F32)
    mid = r.astype(BF16)
    lo = (r - mid.astype(F32)).astype(BF16)
    return hi, mid, lo


def _dot(a, b):
    return jnp.dot(a, b, preferred_element_type=F32)


def _dot_nt(a, b):
    return lax.dot_general(a, b, (((1,), (1,)), ((), ())), preferred_element_type=F32)


def _expand(parts, e):
    acc = _dot(parts[0], e)
    for p in parts[1:]:
        acc = acc + _dot(p, e)
    return acc


def _norm_matmul_body(x_ref, g_ref, w_ref, o_ref, h_ref):
    @pl.when(pl.program_id(1) == 0)
    def _():
        x = x_ref[...]
        r = lax.rsqrt(jnp.mean(x * x, axis=-1, keepdims=True) + EPS)
        h_ref[...] = ((x * r) * g_ref[...]).astype(BF16)

    o_ref[...] = _dot(h_ref[...], w_ref[...])


def _norm_matmul(x, g, w, tm, tn):
    m, d = x.shape
    n = w.shape[1]
    return pl.pallas_call(
        _norm_matmul_body,
        grid=(m // tm, n // tn),
        in_specs=[pl.BlockSpec((tm, d), lambda i, j: (i, 0)),
                  pl.BlockSpec((1, d), lambda i, j: (0, 0)),
                  pl.BlockSpec((d, tn), lambda i, j: (0, j))],
        out_specs=pl.BlockSpec((tm, tn), lambda i, j: (i, j)),
        out_shape=jax.ShapeDtypeStruct((m, n), F32),
        scratch_shapes=[pltpu.VMEM((tm, d), BF16)],
        compiler_params=_params("parallel", "arbitrary"),
        name="norm_matmul",
    )(x, g.reshape(1, d), w)


def _out_proj_body(*refs, n_in, final):
    ins = refs[:n_in]
    ws = refs[n_in:2 * n_in]
    x_ref = refs[2 * n_in]
    g_ref = refs[2 * n_in + 1]
    o_ref = refs[2 * n_in + 2]
    acc = x_ref[...]
    for a, w in zip(ins, ws):
        acc = acc + _dot(a[...], w[...])
    if final:
        r = lax.rsqrt(jnp.mean(acc * acc, axis=-1, keepdims=True) + EPS)
        acc = (acc * r) * g_ref[...]
    o_ref[...] = acc


def _out_proj(parts, w, x, norm_f, final, tm):
    m, d = x.shape
    widths = [p.shape[1] for p in parts]
    offs = np.cumsum([0] + widths).tolist()
    ws = [w[offs[i]:offs[i + 1]] for i in range(len(parts))]
    in_specs = ([pl.BlockSpec((tm, wd), lambda i: (i, 0)) for wd in widths]
                + [pl.BlockSpec((wd, d), lambda i: (0, 0)) for wd in widths]
                + [pl.BlockSpec((tm, d), lambda i: (i, 0)), pl.BlockSpec((1, d), lambda i: (0, 0))])
    return pl.pallas_call(
        functools.partial(_out_proj_body, n_in=len(parts), final=final),
        grid=(m // tm,),
        in_specs=in_specs,
        out_specs=pl.BlockSpec((tm, d), lambda i: (i, 0)),
        out_shape=jax.ShapeDtypeStruct((m, d), F32),
        compiler_params=_params("parallel"),
        name="out_proj",
    )(*parts, *ws, x, norm_f.reshape(1, d))


def _rope_tables(pos):
    half = ROT_DIM // 2
    inv_freq = ROPE_THETA ** (-jnp.arange(0, ROT_DIM, 2, dtype=F32) / ROT_DIM)
    ang = pos.astype(F32)[:, None] * inv_freq[None, :]
    cos, sin = jnp.cos(ang), jnp.sin(ang)
    n = pos.shape[0]
    zeros = jnp.zeros((n, DIFF_DK - ROT_DIM), F32)
    z8 = jnp.zeros((n, half), F32)
    c = jnp.concatenate([cos, cos, zeros + 1.0], axis=-1)
    sa = jnp.concatenate([-sin, z8, zeros], axis=-1)
    sb = jnp.concatenate([z8, sin, zeros], axis=-1)
    return (jnp.concatenate([c, c], axis=-1), jnp.concatenate([sa, sa], axis=-1),
            jnp.concatenate([sb, sb], axis=-1))


def _rope_body(q_ref, k_ref, v_ref, c_ref, sa_ref, sb_ref, qt_ref, kb_ref, kt_ref, vt_ref):
    c, sa, sb = c_ref[...], sa_ref[...], sb_ref[...]
    half = ROT_DIM // 2

    def rot(x):
        return x * c + pltpu.roll(x, LANES - half, 1) * sa + pltpu.roll(x, half, 1) * sb

    for h in range(DIFF_HEADS):
        sl = slice(h * LANES, (h + 1) * LANES)
        kr = rot(k_ref[:, sl])
        kb_ref[:, sl] = kr.astype(BF16)
        kt_ref[sl, :] = kr.T
        qt_ref[sl, :] = (rot(q_ref[:, sl]) * (DIFF_DK ** -0.5)).T.astype(BF16)
        vt_ref[sl, :] = v_ref[:, sl].T.astype(BF16)


def _rope(proj, tables, nb, s_len, tm):
    t = proj.shape[0]
    w = DIFF_QK
    nt = s_len // tm
    tab_spec = pl.BlockSpec((tm, LANES), lambda i: (i % nt, 0))
    row = lambda cb: pl.BlockSpec((tm, w), lambda i: (i, cb))
    col = pl.BlockSpec((w, tm), lambda i: (0, i))
    return pl.pallas_call(
        _rope_body,
        grid=(t // tm,),
        in_specs=[row(1), row(2), row(3), tab_spec, tab_spec, tab_spec],
        out_specs=[col, row(0), pl.BlockSpec((None, w, tm), lambda i: (i // nt, 0, i % nt)), col],
        out_shape=[jax.ShapeDtypeStruct((w, t), BF16), jax.ShapeDtypeStruct((t, w), BF16),
                   jax.ShapeDtypeStruct((nb, w, s_len), F32), jax.ShapeDtypeStruct((w, t), BF16)],
        compiler_params=_params("parallel"),
        name="rope",
    )(proj, proj, proj, *tables)


def _diff_lambda(lam_ref, lam_init):
    lv = lam_ref[...]
    a = jnp.sum(lv[0:1] * lv[1:2], axis=-1, keepdims=True)
    b = jnp.sum(lv[2:3] * lv[3:4], axis=-1, keepdims=True)
    return jnp.exp(a) - jnp.exp(b) + lam_init


def _subln_gate(o, sub_g, g, lam_init):
    r = lax.rsqrt(jnp.mean(o * o, axis=-1, keepdims=True) + EPS)
    return ((o * r) * sub_g) * (1.0 - lam_init) * _silu(g)


def _sublane_allreduce(x, op):
    for sh in (1, 2, 4):
        x = op(x, pltpu.roll(x, sh, 0))
    return x


def _diff_attn_body(qi_ref, ki_ref, qt_ref, k_ref, vt_ref, g_ref, lam_ref, sg_ref, o_ref,
                    m_ref, l_ref, acc_ref, *, t, lam_init):
    pidx = pl.program_id(1)
    qi = qi_ref[pidx]
    ki = ki_ref[pidx]
    ng = t // SUBLANES

    @pl.when(ki == 0)
    def _():
        m_ref[...] = jnp.full(m_ref.shape, NEG_BIG, F32)
        l_ref[...] = jnp.zeros(l_ref.shape, F32)
        acc_ref[...] = jnp.zeros(acc_ref.shape, F32)

    def head(h, masked):
        r0 = pl.multiple_of(h * LANES, LANES)
        qt = qt_ref[pl.ds(r0, LANES), :]
        k = k_ref[:, pl.ds(r0, LANES)]
        vt = vt_ref[pl.ds(r0, LANES), :]
        vt1 = jnp.concatenate([vt, jnp.ones((2 * SUBLANES, t), BF16)], axis=0)
        sub = lax.broadcasted_iota(jnp.int32, qt.shape, 0)
        zero = jnp.zeros_like(qt)
        s2 = [_dot(k, jnp.where(sub < DIFF_DK, qt, zero)), _dot(k, jnp.where(sub >= DIFF_DK, qt, zero))]
        for c in range(2):
            s = s2[c]
            if masked:
                kpos = lax.broadcasted_iota(jnp.int32, s.shape, 0)
                qpos = lax.broadcasted_iota(jnp.int32, s.shape, 1)
                s = jnp.where(kpos <= qpos, s, NEG_BIG)
            s3 = s.reshape(ng, SUBLANES, t)
            m_old = m_ref[h, c]
            m_new = jnp.maximum(m_old, _sublane_allreduce(jnp.max(s3, axis=0), jnp.maximum))
            alpha = jnp.exp(m_old - m_new)
            pb = jnp.exp((s3 - m_new[None]).reshape(t, t).astype(BF16))
            pv = _dot(vt1, pb)
            l_ref[h, c] = alpha * l_ref[h, c] + pv[DIFF_DV:DIFF_DV + SUBLANES]
            acc = acc_ref[h, c].reshape(DIFF_DV // SUBLANES, SUBLANES, t) * alpha[None]
            acc_ref[h, c] = acc.reshape(DIFF_DV, t) + pv[:DIFF_DV]
            m_ref[h, c] = m_new

    def finish(h):
        r0 = pl.multiple_of(h * LANES, LANES)
        lam = _diff_lambda(lam_ref, lam_init)
        parts = []
        for c in range(2):
            parts.append(acc_ref[h, c].reshape(DIFF_DV // SUBLANES, SUBLANES, t) / l_ref[h, c][None])
        o = (parts[0] - lam * parts[1]).reshape(DIFF_DV, t).T
        o_ref[:, pl.ds(r0, LANES)] = _subln_gate(o, sg_ref[...], g_ref[:, pl.ds(r0, LANES)], lam_init).astype(BF16)

    @pl.when(ki < qi)
    def _():
        def body(h, carry):
            head(h, False)
            return carry
        lax.fori_loop(0, DIFF_HEADS, body, 0, unroll=2)

    @pl.when(ki == qi)
    def _():
        def body(h, carry):
            head(h, True)
            finish(h)
            return carry
        lax.fori_loop(0, DIFF_HEADS, body, 0, unroll=2)


def _diff_attn_prompt(qt, kb, vt, proj, lam_vecs, sub_g, lam_init, nb, s_len, t):
    tt = kb.shape[0]
    nq = s_len // t
    pairs = [(i, j) for i in range(nq) for j in range(i + 1)]
    qi_tab = jnp.asarray([a for a, _ in pairs], jnp.int32)
    ki_tab = jnp.asarray([b for _, b in pairs], jnp.int32)
    g_cb = (SSD_INNER + 2 * DIFF_QK + DIFF_WIDTH) // DIFF_WIDTH
    w = DIFF_WIDTH
    grid_spec = pltpu.PrefetchScalarGridSpec(
        num_scalar_prefetch=2,
        grid=(nb, len(pairs)),
        in_specs=[pl.BlockSpec((w, t), lambda b, p, qtab, ktab: (0, b * nq + qtab[p])),
                  pl.BlockSpec((t, w), lambda b, p, qtab, ktab: (b * nq + ktab[p], 0)),
                  pl.BlockSpec((w, t), lambda b, p, qtab, ktab: (0, b * nq + ktab[p])),
                  pl.BlockSpec((t, w), lambda b, p, qtab, ktab: (b * nq + qtab[p], g_cb)),
                  pl.BlockSpec((4, DIFF_DK), lambda b, p, qtab, ktab: (0, 0)),
                  pl.BlockSpec((1, DIFF_DV), lambda b, p, qtab, ktab: (0, 0))],
        out_specs=pl.BlockSpec((t, w), lambda b, p, qtab, ktab: (b * nq + qtab[p], 0)),
        scratch_shapes=[pltpu.VMEM((DIFF_HEADS, 2, SUBLANES, t), F32), pltpu.VMEM((DIFF_HEADS, 2, SUBLANES, t), F32),
                        pltpu.VMEM((DIFF_HEADS, 2, DIFF_DV, t), F32)],
    )
    return pl.pallas_call(
        functools.partial(_diff_attn_body, t=t, lam_init=lam_init),
        grid_spec=grid_spec,
        out_shape=jax.ShapeDtypeStruct((tt, w), BF16),
        compiler_params=_params("parallel", "arbitrary"),
        name="diff_attn_prompt",
    )(qi_tab, ki_tab, qt, kb, vt, proj, lam_vecs, sub_g.reshape(1, DIFF_DV))


PAGES_PER_STEP = 8


def _diff_attn_sample_body(pt_ref, qt_ref, knt_ref, vn_ref, g_ref, *rest, lam_init):
    npg = PAGES_PER_STEP
    kt_refs = rest[:npg]
    v_refs = rest[npg:2 * npg]
    lam_ref, sg_ref, o_ref, qcol_ref, m_ref, l_ref, acc_ref = rest[2 * npg:]
    b = pl.program_id(0)
    p = pl.program_id(1)
    nreq = qt_ref.shape[1]
    shape4 = (DIFF_HEADS, 2, DIFF_DK, PAGE_SIZE)

    @pl.when(p == 0)
    def _():
        onehot = (lax.broadcasted_iota(jnp.int32, (nreq, LANES), 0) == b).astype(BF16)
        qcol = _dot(qt_ref[...], onehot)
        qcol_ref[...] = qcol
        kn_hi, kn_lo = _split2(knt_ref[...])
        kcol = _dot(kn_hi, onehot) + _dot(kn_lo, onehot)
        prod = (qcol * kcol).reshape(shape4)
        for c in range(2):
            m_ref[c] = jnp.sum(prod[:, c], axis=1)
            l_ref[c] = jnp.ones((DIFF_HEADS, LANES), F32)
            acc_ref[c] = vn_ref[...]

    lane = lax.broadcasted_iota(jnp.int32, (DIFF_HEADS, LANES), 1)
    sub_w = lax.broadcasted_iota(jnp.int32, (DIFF_HEADS, PAGE_SIZE * DIFF_HEADS), 0)
    lane_w = lax.broadcasted_iota(jnp.int32, (DIFF_HEADS, PAGE_SIZE * DIFF_HEADS), 1)
    own_head = sub_w == lane_w % DIFF_HEADS
    keys_per_block = LANES // DIFF_HEADS
    for i in range(npg):
        prod = (qcol_ref[...] * kt_refs[i][...]).reshape(shape4)
        v2 = v_refs[i][...].reshape(PAGE_SIZE * DIFF_HEADS, DIFF_DV).astype(BF16)
        ws, alphas = [], []
        for c in range(2):
            s = jnp.sum(prod[:, c], axis=1)
            m_old = m_ref[c]
            m_new = jnp.maximum(m_old, jnp.max(s, axis=-1, keepdims=True))
            alpha = jnp.exp(m_old - m_new)
            pr = jnp.exp(s - m_new)
            l_ref[c] = alpha * l_ref[c] + jnp.sum(pr, axis=-1, keepdims=True)
            m_ref[c] = m_new
            wexp = jnp.concatenate(
                [jnp.take_along_axis(pr, keys_per_block * j + lane // DIFF_HEADS, axis=1)
                 for j in range(DIFF_HEADS)], axis=1)
            ws.append(jnp.where(own_head, wexp, 0.0))
            alphas.append(alpha)
        pv = _dot(jnp.concatenate(ws, axis=0).astype(BF16), v2)
        for c in range(2):
            acc_ref[c] = alphas[c] * acc_ref[c] + pv[c * DIFF_HEADS:(c + 1) * DIFF_HEADS]

    @pl.when(p == pl.num_programs(1) - 1)
    def _():
        lam = _diff_lambda(lam_ref, lam_init)
        o = acc_ref[0] / l_ref[0] - lam * (acc_ref[1] / l_ref[1])
        o_ref[...] = _subln_gate(o, sg_ref[...], g_ref[...], lam_init).astype(BF16)


def _diff_attn_sample(qt, kn_t, proj, cache_kt, cache_v, n_pool, page_table, layer_a, lam_vecs, sub_g, lam_init):
    nreq = kn_t.shape[1]
    n_pages = page_table.shape[1]
    npg = PAGES_PER_STEP
    assert n_pages % npg == 0
    base = layer_a * n_pool
    proj3 = proj.reshape(nreq, proj.shape[1] // DIFF_DV, DIFF_DV)
    v_cb = (SSD_INNER + 2 * DIFF_QK) // DIFF_WIDTH
    head_rows = lambda cb: pl.BlockSpec((None, DIFF_HEADS, DIFF_DV), lambda b, p, pt: (b, cb, 0))
    const = lambda shape: pl.BlockSpec(shape, lambda b, p, pt: (0,) * len(shape))

    def page_spec(shape, i):
        zeros = (0,) * (len(shape) - 1)
        return pl.BlockSpec(shape, lambda b, p, pt: (base + pt[b * n_pages + p * npg + i],) + zeros)

    grid_spec = pltpu.PrefetchScalarGridSpec(
        num_scalar_prefetch=1,
        grid=(nreq, n_pages // npg),
        in_specs=([const((DIFF_QK, nreq)), const((DIFF_QK, nreq)), head_rows(v_cb), head_rows(v_cb + 1)]
                  + [page_spec((None, DIFF_QK, PAGE_SIZE), i) for i in range(npg)]
                  + [page_spec((None, PAGE_SIZE, DIFF_HEADS, DIFF_DV), i) for i in range(npg)]
                  + [const((4, DIFF_DK)), const((1, DIFF_DV))]),
        out_specs=head_rows(0),
        scratch_shapes=[pltpu.VMEM((DIFF_QK, LANES), F32), pltpu.VMEM((2, DIFF_HEADS, LANES), F32),
                        pltpu.VMEM((2, DIFF_HEADS, LANES), F32), pltpu.VMEM((2, DIFF_HEADS, DIFF_DV), F32)],
    )
    out = pl.pallas_call(
        functools.partial(_diff_attn_sample_body, lam_init=lam_init),
        grid_spec=grid_spec,
        out_shape=jax.ShapeDtypeStruct((nreq, DIFF_HEADS, DIFF_DV), BF16),
        compiler_params=_params("parallel", "arbitrary"),
        name="diff_attn_sample",
    )(page_table.reshape(-1), qt, kn_t, proj3, proj3, *([cache_kt] * npg), *([cache_v] * npg), lam_vecs,
      sub_g.reshape(1, DIFF_DV))
    return out.reshape(nreq, DIFF_WIDTH)


def _xattn_heads(xq, xg, mk, mv):
    outs = []
    for h in range(X_HEADS):
        sl = slice(h * X_HEADDIM, (h + 1) * X_HEADDIM)
        s = _dot_nt(xq[:, sl].astype(BF16), mk[:, sl].astype(BF16)) * (X_HEADDIM ** -0.5)
        e = jnp.exp(s - jnp.max(s, axis=-1, keepdims=True))
        pr = (e / jnp.sum(e, axis=-1, keepdims=True)).astype(BF16)
        outs.append(_dot(pr, mv[:, sl].astype(BF16)) * _silu(xg[:, sl]))
    return outs


def _xattn_prompt_body(xq_ref, xg_ref, mk_ref, mv_ref, o_ref):
    outs = _xattn_heads(xq_ref[...], xg_ref[...], mk_ref[...], mv_ref[...])
    for h in range(X_HEADS):
        o_ref[:, h * X_HEADDIM:(h + 1) * X_HEADDIM] = outs[h].astype(BF16)


def _xattn_prompt(proj, xq_cb, mem_kv, layer, nb, s_len, tq):
    t = proj.shape[0]
    nq = s_len // tq
    return pl.pallas_call(
        _xattn_prompt_body,
        grid=(nb, nq),
        in_specs=[pl.BlockSpec((tq, X_WIDTH), lambda b, i: (b * nq + i, xq_cb)),
                  pl.BlockSpec((tq, X_WIDTH), lambda b, i: (b * nq + i, xq_cb + 1)),
                  pl.BlockSpec((None, N_MEM, X_WIDTH), lambda b, i: (b, 0, 2 * layer)),
                  pl.BlockSpec((None, N_MEM, X_WIDTH), lambda b, i: (b, 0, 2 * layer + 1))],
        out_specs=pl.BlockSpec((tq, X_WIDTH), lambda b, i: (b * nq + i, 0)),
        out_shape=jax.ShapeDtypeStruct((t, X_WIDTH), BF16),
        compiler_params=_params("parallel", "parallel"),
        name="xattn_prompt",
    )(proj, proj, mem_kv, mem_kv)


def _xattn_sample_body(xqg_ref, mk_ref, mv_ref, o_ref, *, rb):
    shape = (2 * X_HEADS, N_MEM * X_HEADS)
    own_head = lax.broadcasted_iota(jnp.int32, shape, 0) == lax.broadcasted_iota(jnp.int32, shape, 1) % X_HEADS
    scores = []
    for r in range(rb):
        s = _dot_nt(xqg_ref[r].astype(BF16), mk_ref[r].astype(BF16)) * (X_HEADDIM ** -0.5)
        scores.append(jnp.where(own_head, s, NEG_BIG))
    probs = []
    for s in scores:
        e = jnp.exp(s - jnp.max(s, axis=-1, keepdims=True))
        probs.append((e / jnp.sum(e, axis=-1, keepdims=True)).astype(BF16))
    for r in range(rb):
        o = _dot(probs[r], mv_ref[r].astype(BF16))[0:X_HEADS]
        o_ref[r] = (o * _silu(xqg_ref[r, X_HEADS:2 * X_HEADS, :])).astype(BF16)


def _xattn_sample(proj, xq_cb, mem_k, mem_v, layer, rb):
    nreq = proj.shape[0]
    p3 = proj.reshape(nreq, proj.shape[1] // X_HEADDIM, X_HEADDIM)
    mem_spec = pl.BlockSpec((None, rb, N_MEM * X_HEADS, X_HEADDIM), lambda i: (layer, i, 0, 0))
    out = pl.pallas_call(
        functools.partial(_xattn_sample_body, rb=rb),
        grid=(nreq // rb,),
        in_specs=[pl.BlockSpec((rb, 2 * X_HEADS, X_HEADDIM), lambda i: (i, xq_cb // 2, 0)), mem_spec, mem_spec],
        out_specs=pl.BlockSpec((rb, X_HEADS, X_HEADDIM), lambda i: (i, 0, 0)),
        out_shape=jax.ShapeDtypeStruct((nreq, X_HEADS, X_HEADDIM), BF16),
        compiler_params=_params("parallel"),
        name="xattn_sample",
    )(p3, mem_k, mem_v)
    return out.reshape(nreq, X_WIDTH)


def _ssd_consts(dt_bias, a_log, d_skip):
    pad = LANES - SSD_HEADS
    dtb = jnp.pad(dt_bias.astype(F32), (0, pad)).reshape(1, LANES)
    alog = jnp.pad(a_log.astype(F32), (0, pad)).reshape(1, LANES)
    dexp = jnp.repeat(d_skip.astype(F32), SSD_HEADDIM).reshape(1, SSD_INNER)
    head = np.arange(LANES)[:, None]
    chan_head = np.arange(SSD_INNER)[None, :] // SSD_HEADDIM
    eexp = jnp.asarray((head == chan_head).astype(np.float32), BF16)
    return dtb, alog, dexp, eexp


def _softplus(x):
    return jnp.maximum(x, 0.0) + jnp.log1p(jnp.exp(-jnp.abs(x)))


def _ssd_gate_norm(y, xs, z, dexp, ng):
    y = (y + dexp * xs) * _silu(z)
    r = lax.rsqrt(jnp.mean(y * y, axis=-1, keepdims=True) + EPS)
    return ((y * r) * ng).astype(BF16)


def _ssd_prompt_body(z_ref, xbc_ref, dtr_ref, cw_ref, cb_ref, dtb_ref, alog_ref, dexp_ref, ng_ref, eexp_ref,
                     y_ref, hout_ref, ext_ref, h_ref, yd_ref):
    q = SSD_CHUNK
    c = pl.program_id(1)

    @pl.when(c == 0)
    def _():
        ext_ref[0:SUBLANES, :] = jnp.zeros((SUBLANES, SSD_CONV_CH), F32)
        h_ref[...] = jnp.zeros(h_ref.shape, F32)

    x_raw = xbc_ref[...]
    ext_ref[SUBLANES:SUBLANES + q, :] = x_raw
    conv = cb_ref[...] + cw_ref[SSD_CONV - 1:SSD_CONV, :] * x_raw
    for k in range(1, SSD_CONV):
        conv = conv + cw_ref[SSD_CONV - 1 - k:SSD_CONV - k, :] * ext_ref[SUBLANES - k:SUBLANES - k + q, :]
    ext_ref[0:SUBLANES, :] = x_raw[q - SUBLANES:q, :]
    xbc = _silu(conv)
    xs = xbc[:, :SSD_INNER]
    bm = xbc[:, SSD_INNER:SSD_INNER + SSD_GROUPS * SSD_STATE].astype(BF16)
    cm = xbc[:, SSD_INNER + SSD_GROUPS * SSD_STATE:].astype(BF16)

    eexp = eexp_ref[...]
    dt = _softplus(dtr_ref[...] + dtb_ref[...])
    da = dt * (-jnp.exp(alog_ref[...]))
    ri = lax.broadcasted_iota(jnp.int32, (q, q), 0)
    ci = lax.broadcasted_iota(jnp.int32, (q, q), 1)
    tri = ri >= ci
    tri_b = tri.astype(BF16)
    hi, mid, lo = _split3(da)
    acum = _dot(tri_b, hi) + _dot(tri_b, mid) + _dot(tri_b, lo)
    acum_t = acum.T
    a_last = acum[q - 1:q, :]
    xdt = xs * _expand(_split2(dt), eexp)
    xdt_b = xdt.astype(BF16)
    ea_exp = _expand(_split2(jnp.exp(acum)), eexp)
    wend_exp = _expand(_split2(jnp.exp(a_last - acum)), eexp)
    xw_t = (xdt * wend_exp).T.astype(BF16)
    decay = jnp.exp(a_last)

    lane = lax.broadcasted_iota(jnp.int32, (q, LANES), 1)
    for g in range(SSD_GROUPS):
        gs = slice(g * SSD_STATE, (g + 1) * SSD_STATE)
        ch = slice(g * SSD_HPG * SSD_HEADDIM, (g + 1) * SSD_HPG * SSD_HEADDIM)
        scores = _dot_nt(cm[:, gs], bm[:, gs])
        for pair in range(SSD_HPG // 2):
            halves = []
            for r in range(2):
                hd = g * SSD_HPG + 2 * pair + r
                seg = acum[:, hd:hd + 1] - acum_t[hd:hd + 1, :]
                ldec = jnp.where(tri, jnp.exp(jnp.where(tri, seg, 0.0)), 0.0)
                xp = xdt_b[:, (hd // 2) * LANES:(hd // 2 + 1) * LANES]
                halves.append(_dot((scores * ldec).astype(BF16), xp))
            lo_lane = (g * SSD_HPG + 2 * pair) // 2 * LANES
            yd_ref[:, lo_lane:lo_lane + LANES] = jnp.where(lane < SSD_HEADDIM, halves[0], halves[1])
        h_g = h_ref[ch, :]
        yd_ref[:, ch] = yd_ref[:, ch] + _dot_nt(cm[:, gs], h_g.astype(BF16)) * ea_exp[:, ch]
        upd = _dot(xw_t[ch, :], bm[:, gs])
        for r in range(SSD_HPG):
            hd = g * SSD_HPG + r
            rows = slice(hd * SSD_HEADDIM, (hd + 1) * SSD_HEADDIM)
            h_ref[rows, :] = h_ref[rows, :] * decay[:, hd:hd + 1] + upd[r * SSD_HEADDIM:(r + 1) * SSD_HEADDIM, :]

    y_ref[...] = _ssd_gate_norm(yd_ref[...], xs, z_ref[...], dexp_ref[...], ng_ref[...])
    hout_ref[...] = h_ref[...]


def _ssd_prompt(proj, proj_s, conv_w, conv_b, consts, norm_g, nb, s_len):
    t = proj.shape[0]
    q = SSD_CHUNK
    nc = s_len // q
    dtb, alog, dexp, eexp = consts
    const = lambda shape: pl.BlockSpec(shape, lambda b, c: (0,) * len(shape))
    return pl.pallas_call(
        _ssd_prompt_body,
        grid=(nb, nc),
        in_specs=[pl.BlockSpec((q, SSD_INNER), lambda b, c: (b * nc + c, 0)),
                  pl.BlockSpec((q, SSD_CONV_CH), lambda b, c: (b * nc + c, 0)),
                  pl.BlockSpec((q, LANES), lambda b, c: (b * nc + c, SSD_CONV_CH // LANES)),
                  const((SSD_CONV, SSD_CONV_CH)), const((1, SSD_CONV_CH)), const((1, LANES)), const((1, LANES)),
                  const((1, SSD_INNER)), const((1, SSD_INNER)), const((LANES, SSD_INNER))],
        out_specs=[pl.BlockSpec((q, SSD_INNER), lambda b, c: (b * nc + c, 0)),
                   pl.BlockSpec((None, SSD_INNER, SSD_STATE), lambda b, c: (b, 0, 0))],
        out_shape=[jax.ShapeDtypeStruct((t, SSD_INNER), BF16),
                   jax.ShapeDtypeStruct((nb, SSD_INNER, SSD_STATE), F32)],
        scratch_shapes=[pltpu.VMEM((SUBLANES + q, SSD_CONV_CH), F32), pltpu.VMEM((SSD_INNER, SSD_STATE), F32),
                        pltpu.VMEM((q, SSD_INNER), F32)],
        compiler_params=_params("parallel", "arbitrary"),
        name="ssd_prompt",
    )(proj, proj_s, proj_s, conv_w, conv_b.reshape(1, -1), dtb, alog, dexp, norm_g.reshape(1, -1), eexp)


def _ssd_sample_body(z_ref, xbc_ref, dtr_ref, cs_ref, hin_ref, cw_ref, cb_ref, dtb_ref, alog_ref, dexp_ref, ng_ref,
                     eexp_ref, y_ref, hout_ref, xs_ref, xdt_t_ref, da_hi_ref, da_lo_ref, bm_ref, cm_t_ref, y_t_ref,
                     *, rb):
    i = pl.program_id(0)
    nreq = z_ref.shape[0]

    @pl.when(i == 0)
    def _():
        conv = cb_ref[...] + cw_ref[SSD_CONV - 1:SSD_CONV, :] * xbc_ref[...]
        for k in range(SSD_CONV - 1):
            conv = conv + cw_ref[k:k + 1, :] * cs_ref[k]
        xbc = _silu(conv)
        xs = xbc[:, :SSD_INNER]
        xs_ref[...] = xs
        bm_ref[...] = xbc[:, SSD_INNER:SSD_INNER + SSD_GROUPS * SSD_STATE]
        cm_t_ref[...] = xbc[:, SSD_INNER + SSD_GROUPS * SSD_STATE:].T
        dt = _softplus(dtr_ref[...] + dtb_ref[...])
        da_hi, da_lo = _split2(jnp.exp(dt * (-jnp.exp(alog_ref[...]))).T)
        da_hi_ref[...] = da_hi
        da_lo_ref[...] = da_lo
        xdt_t_ref[...] = (xs * _expand(_split2(dt), eexp_ref[...])).T.astype(BF16)
        y_t_ref[...] = jnp.zeros(y_t_ref.shape, F32)

    def per_request(r, carry):
        b = i * rb + r
        onehot = (lax.broadcasted_iota(jnp.int32, (nreq, LANES), 0) == b).astype(BF16)
        xb = _dot(xdt_t_ref[...], onehot)
        dab = _dot(da_hi_ref[...], onehot) + _dot(da_lo_ref[...], onehot)
        brow = bm_ref[pl.ds(b, 1), :]
        cmask = lax.broadcasted_iota(jnp.int32, (SSD_STATE, nreq), 1) == b
        for g in range(SSD_GROUPS):
            gs = slice(g * SSD_STATE, (g + 1) * SSD_STATE)
            ch = slice(g * SSD_HPG * SSD_HEADDIM, (g + 1) * SSD_HPG * SSD_HEADDIM)
            for hr in range(SSD_HPG):
                hd = g * SSD_HPG + hr
                rows = slice(hd * SSD_HEADDIM, (hd + 1) * SSD_HEADDIM)
                hout_ref[r, rows, :] = dab[hd:hd + 1, :] * hin_ref[r, rows, :] + xb[rows, :] * brow[:, gs]
            cmat = jnp.where(cmask, cm_t_ref[gs, :], 0.0).astype(BF16)
            y_t_ref[ch, :] = y_t_ref[ch, :] + _dot(hout_ref[r, ch, :].astype(BF16), cmat)
        return carry

    lax.fori_loop(0, rb, per_request, 0)

    @pl.when(i == pl.num_programs(0) - 1)
    def _():
        y_ref[...] = _ssd_gate_norm(y_t_ref[...].T, xs_ref[...], z_ref[...], dexp_ref[...], ng_ref[...])


def _ssd_sample(proj, proj_s, conv_state_t, h_state, layer_a, conv_w, conv_b, consts, norm_g, rb):
    nreq = proj.shape[0]
    dtb, alog, dexp, eexp = consts
    const = lambda shape: pl.BlockSpec(shape, lambda i: (0,) * len(shape))
    return pl.pallas_call(
        functools.partial(_ssd_sample_body, rb=rb),
        grid=(nreq // rb,),
        in_specs=[pl.BlockSpec((nreq, SSD_INNER), lambda i: (0, 0)),
                  pl.BlockSpec((nreq, SSD_CONV_CH), lambda i: (0, 0)),
                  pl.BlockSpec((nreq, LANES), lambda i: (0, SSD_CONV_CH // LANES)),
                  const((SSD_CONV - 1, nreq, SSD_CONV_CH)),
                  pl.BlockSpec((None, rb, SSD_INNER, SSD_STATE), lambda i: (layer_a, i, 0, 0)),
                  const((SSD_CONV, SSD_CONV_CH)), const((1, SSD_CONV_CH)), const((1, LANES)), const((1, LANES)),
                  const((1, SSD_INNER)), const((1, SSD_INNER)), const((LANES, SSD_INNER))],
        out_specs=[const((nreq, SSD_INNER)),
                   pl.BlockSpec((rb, SSD_INNER, SSD_STATE), lambda i: (i, 0, 0))],
        out_shape=[jax.ShapeDtypeStruct((nreq, SSD_INNER), BF16),
                   jax.ShapeDtypeStruct(h_state.shape[1:], F32)],
        scratch_shapes=[pltpu.VMEM((nreq, SSD_INNER), F32), pltpu.VMEM((SSD_INNER, nreq), BF16),
                        pltpu.VMEM((LANES, nreq), BF16), pltpu.VMEM((LANES, nreq), BF16),
                        pltpu.VMEM((nreq, SSD_GROUPS * SSD_STATE), F32),
                        pltpu.VMEM((SSD_GROUPS * SSD_STATE, nreq), F32), pltpu.VMEM((SSD_INNER, nreq), F32)],
        compiler_params=_params("arbitrary"),
        name="ssd_sample",
    )(proj, proj_s, proj_s, conv_state_t, h_state, conv_w, conv_b.reshape(1, -1), dtb, alog, dexp,
      norm_g.reshape(1, -1), eexp)


CONF_PAD = 32
CONF_RB = 128
CONF_CB = LANES
CONF_NCB = CONF_WIDTH // CONF_CB


def _conf_finish(c, g, ln_g, ln_b):
    mu = jnp.mean(c, axis=-1, keepdims=True)
    xc = c - mu
    r = lax.rsqrt(jnp.mean(xc * xc, axis=-1, keepdims=True) + EPS)
    return (_silu((xc * r) * ln_g + ln_b) * _silu(g)).astype(BF16)


def _conf_prompt_body(a_ref, ag_ref, g_ref, w_ref, b_ref, lg_ref, lb_ref, y_ref, tail_ref, sh_ref, c_ref, *, tt):
    i = pl.program_id(1)
    rows = CONF_PAD + tt

    @pl.when(i == 0)
    def _():
        sh_ref[0, :, 0:CONF_PAD, :] = jnp.zeros((CONF_NCB, CONF_PAD, CONF_CB), F32)

    def glu(cb, carry):
        c0 = pl.multiple_of(cb * CONF_CB, CONF_CB)
        u = a_ref[:, pl.ds(c0, CONF_CB)] / (1.0 + jnp.exp(-ag_ref[:, pl.ds(c0, CONF_CB)]))
        tail_ref[:, pl.ds(c0, CONF_CB)] = u[tt - CONF_PAD:tt, :]
        sh_ref[0, cb, CONF_PAD:rows, :] = u
        for s in range(1, SUBLANES):
            sh_ref[s, cb, 0:rows - SUBLANES, :] = sh_ref[0, cb, s:s + rows - SUBLANES, :]
        return carry

    lax.fori_loop(0, CONF_NCB, glu, 0)
    first = CONF_PAD - (CONF_CONV_W - 1)

    def tile(idx, carry):
        rb = idx // CONF_NCB
        cb = idx % CONF_NCB
        r0 = pl.multiple_of(rb * CONF_RB, CONF_RB)
        acc = jnp.broadcast_to(b_ref[cb], (CONF_RB, CONF_CB))
        for k in range(CONF_CONV_W):
            off = first + k
            src = sh_ref[off % SUBLANES, cb, pl.ds(r0 + (off // SUBLANES) * SUBLANES, CONF_RB), :]
            acc = acc + w_ref[k, cb] * src
        c_ref[cb, pl.ds(r0, CONF_RB), :] = acc
        return carry

    lax.fori_loop(0, (tt // CONF_RB) * CONF_NCB, tile, 0)

    def lane_mean(x):
        return jnp.broadcast_to(jnp.sum(x, axis=-1, keepdims=True) * (1.0 / CONF_WIDTH), x.shape)

    zero = jnp.zeros((tt, CONF_CB), F32)
    mu = lane_mean(lax.fori_loop(0, CONF_NCB, lambda cb, acc: acc + c_ref[cb], zero))
    var = lane_mean(lax.fori_loop(0, CONF_NCB, lambda cb, acc: acc + (c_ref[cb] - mu) * (c_ref[cb] - mu), zero))
    r = lax.rsqrt(var + EPS)

    def finish(cb, carry):
        c0 = pl.multiple_of(cb * CONF_CB, CONF_CB)
        sh_ref[0, cb, 0:CONF_PAD, :] = sh_ref[0, cb, tt:rows, :]
        y = _silu(((c_ref[cb] - mu) * r) * lg_ref[cb] + lb_ref[cb]) * _silu(g_ref[:, pl.ds(c0, CONF_CB)])
        y_ref[:, pl.ds(c0, CONF_CB)] = y.astype(BF16)
        return carry

    lax.fori_loop(0, CONF_NCB, finish, 0)


def _conf_prompt(proj, dw_w, dw_b, ln_g, ln_b, nb, s_len, tt):
    t = proj.shape[0]
    nt = s_len // tt
    col = lambda cb: pl.BlockSpec((tt, CONF_WIDTH), lambda b, i: (b * nt + i, cb))
    const = lambda shape: pl.BlockSpec(shape, lambda b, i: (0,) * len(shape))
    return pl.pallas_call(
        functools.partial(_conf_prompt_body, tt=tt),
        grid=(nb, nt),
        in_specs=[col(0), col(1), col(2), const((CONF_CONV_W, CONF_NCB, 1, CONF_CB)), const((CONF_NCB, 1, CONF_CB)),
                  const((CONF_NCB, 1, CONF_CB)), const((CONF_NCB, 1, CONF_CB))],
        out_specs=[pl.BlockSpec((tt, CONF_WIDTH), lambda b, i: (b * nt + i, 0)),
                   pl.BlockSpec((None, CONF_PAD, CONF_WIDTH), lambda b, i: (b, 0, 0))],
        out_shape=[jax.ShapeDtypeStruct((t, CONF_WIDTH), BF16),
                   jax.ShapeDtypeStruct((nb, CONF_PAD, CONF_WIDTH), F32)],
        scratch_shapes=[pltpu.VMEM((SUBLANES, CONF_NCB, CONF_PAD + tt, CONF_CB), F32),
                        pltpu.VMEM((CONF_NCB, tt, CONF_CB), F32)],
        compiler_params=_params("parallel", "arbitrary"),
        name="conf_prompt",
    )(proj, proj, proj, dw_w.reshape(CONF_CONV_W, CONF_NCB, 1, CONF_CB), dw_b.reshape(CONF_NCB, 1, CONF_CB),
      ln_g.reshape(CONF_NCB, 1, CONF_CB), ln_b.reshape(CONF_NCB, 1, CONF_CB))


def _conf_sample_body(a_ref, ag_ref, g_ref, st_ref, w_ref, b_ref, lg_ref, lb_ref, y_ref, u_ref, c_ref, *, tc):
    j = pl.program_id(0)
    u = a_ref[...] / (1.0 + jnp.exp(-ag_ref[...]))
    u_ref[...] = u
    acc = b_ref[...] + w_ref[CONF_CONV_W - 1:CONF_CONV_W, :] * u
    for k in range(CONF_CONV_W - 1):
        acc = acc + w_ref[k:k + 1, :] * st_ref[k]
    c_ref[j] = acc

    @pl.when(j == pl.num_programs(0) - 1)
    def _():
        c = jnp.concatenate([c_ref[k] for k in range(CONF_WIDTH // tc)], axis=1)
        y_ref[...] = _conf_finish(c, g_ref[...], lg_ref[...], lb_ref[...])


def _conf_sample(proj, state_t, dw_w, dw_b, ln_g, ln_b, tc):
    nreq = proj.shape[0]
    nblk = CONF_WIDTH // tc
    col = lambda off: pl.BlockSpec((nreq, tc), lambda j: (0, off * nblk + j))
    vec = pl.BlockSpec((1, tc), lambda j: (0, j))
    full = pl.BlockSpec((1, CONF_WIDTH), lambda j: (0, 0))
    return pl.pallas_call(
        functools.partial(_conf_sample_body, tc=tc),
        grid=(nblk,),
        in_specs=[col(0), col(1), pl.BlockSpec((nreq, CONF_WIDTH), lambda j: (0, 2)),
                  pl.BlockSpec((CONF_CONV_W - 1, nreq, tc), lambda j: (0, 0, j)),
                  pl.BlockSpec((CONF_CONV_W, tc), lambda j: (0, j)), vec, full, full],
        out_specs=[pl.BlockSpec((nreq, CONF_WIDTH), lambda j: (0, 0)),
                   pl.BlockSpec((nreq, tc), lambda j: (0, j))],
        out_shape=[jax.ShapeDtypeStruct((nreq, CONF_WIDTH), BF16),
                   jax.ShapeDtypeStruct((nreq, CONF_WIDTH), F32)],
        scratch_shapes=[pltpu.VMEM((nblk, nreq, tc), F32)],
        compiler_params=_params("arbitrary"),
        name="conf_sample",
    )(proj, proj, proj, state_t, dw_w, dw_b.reshape(1, -1), ln_g.reshape(1, -1), ln_b.reshape(1, -1))


def _even_weights(w_in):
    cuts = np.cumsum([SSD_INNER, SSD_CONV_CH, SSD_HEADS]).tolist()
    z, xbc, dt, rest = (w_in[:, :cuts[0]], w_in[:, cuts[0]:cuts[1]], w_in[:, cuts[1]:cuts[2]], w_in[:, cuts[2]:])
    main = jnp.concatenate([z, rest], axis=1).astype(BF16)
    pad = jnp.zeros((w_in.shape[0], LANES - SSD_HEADS), w_in.dtype)
    ssd = jnp.concatenate([xbc, dt, pad], axis=1).astype(BF16)
    return main, ssd


def kernel(x_prompt, x_sample, cache_attn_k, cache_attn_v, cache_mem_k, cache_mem_v, state_ssm, state_ssm_conv,
           state_conf_conv, page_table, mem_prompt, norm_a, w_in_a, ssm_conv_w, ssm_conv_b, ssm_dt_bias, ssm_a_log,
           ssm_d, ssm_norm, diff_lambda, diff_subln, w_out_a, norm_c, w_in_c, conf_dw_w, conf_dw_b, conf_ln_g,
           conf_ln_b, w_out_c, xattn_wk, xattn_wv, mem_norm, norm_f):
    bp, sp, d = x_prompt.shape
    bs = x_sample.shape[0]
    n_a = w_in_a.shape[0]
    n_pool = cache_attn_k.shape[1]
    past = page_table.shape[1] * PAGE_SIZE
    tp = bp * sp

    tm_p = min(1024, sp)
    t_attn = min(1024, sp)
    tm_rope = min(512, sp)
    rb_x = 8
    tq_x = min(512, sp)
    tt_conf = min(256, sp)
    tm_out = min(512, sp)

    xp = x_prompt.reshape(tp, d)
    xs = x_sample.reshape(bs, d)

    w_mem = jnp.concatenate([jnp.concatenate([xattn_wk[l], xattn_wv[l]], axis=1) for l in range(DEPTH)],
                            axis=1).astype(BF16)
    mem_kv = _norm_matmul(mem_prompt.reshape(bp * N_MEM, d), mem_norm, w_mem, N_MEM, 1024)
    mem_kv3 = mem_kv.reshape(bp, N_MEM, DEPTH * 2 * X_WIDTH)
    mem_kv5 = mem_kv.reshape(bp, N_MEM, DEPTH, 2, X_HEADS, X_HEADDIM)
    p_mem_k = jnp.moveaxis(mem_kv5[:, :, :, 0], 2, 0)
    p_mem_v = jnp.moveaxis(mem_kv5[:, :, :, 1], 2, 0)

    tab_p = _rope_tables(jnp.arange(sp))
    tab_s = _rope_tables(jnp.full((bs,), past, jnp.int32))
    cache_kt = jnp.swapaxes(cache_attn_k.reshape(n_a * n_pool, PAGE_SIZE, DIFF_QK), 1, 2)
    cache_v4 = cache_attn_v.reshape(n_a * n_pool, PAGE_SIZE, DIFF_HEADS, DIFF_DV)
    h_state = state_ssm.reshape(n_a, bs, SSD_INNER, SSD_STATE)
    mem_k_rows = cache_mem_k.reshape(DEPTH, bs, N_MEM * X_HEADS, X_HEADDIM)
    mem_v_rows = cache_mem_v.reshape(DEPTH, bs, N_MEM * X_HEADS, X_HEADDIM)

    pk, pv, pssm, pconv, pconf = [], [], [], [], []
    sk, sv, sssm, sconv, sconf = [], [], [], [], []
    for layer in range(DEPTH):
        j = layer // 2
        final = layer == DEPTH - 1
        if layer % 2 == 0:
            lam_init = 0.8 - 0.6 * math.exp(-0.3 * layer)
            w_main, w_ssd = _even_weights(w_in_a[j])
            w_out = w_out_a[j].astype(BF16)
            consts = _ssd_consts(ssm_dt_bias[j], ssm_a_log[j], ssm_d[j])
            xq_cb = (SSD_INNER + 2 * DIFF_QK + 2 * DIFF_WIDTH) // X_WIDTH
            v_lo = SSD_INNER + 2 * DIFF_QK

            proj = _norm_matmul(xp, norm_a[j], w_main, tm_p, 2048)
            proj_s = _norm_matmul(xp, norm_a[j], w_ssd, tm_p, w_ssd.shape[1])
            y_ssd, h_last = _ssd_prompt(proj, proj_s, ssm_conv_w[j], ssm_conv_b[j], consts, ssm_norm[j], bp, sp)
            qt, kb, k_rot_t, vt = _rope(proj, tab_p, bp, sp, tm_rope)
            y_diff = _diff_attn_prompt(qt, kb, vt, proj, diff_lambda[j], diff_subln[j], lam_init, bp, sp, t_attn)
            y_x = _xattn_prompt(proj, xq_cb, mem_kv3, layer, bp, sp, tq_x)
            xp = _out_proj([y_ssd, y_diff, y_x], w_out, xp, norm_f, final, tm_out)
            pk.append(k_rot_t)
            pv.append(proj[:, v_lo:v_lo + DIFF_WIDTH].reshape(bp, sp, DIFF_HEADS, DIFF_DV))
            pssm.append(h_last.reshape(bp, SSD_GROUPS, SSD_HPG, SSD_HEADDIM, SSD_STATE))
            pconv.append(proj_s.reshape(bp, sp, -1)[:, sp - (SSD_CONV - 1):, :SSD_CONV_CH])

            proj = _norm_matmul(xs, norm_a[j], w_main, bs, 1024)
            proj_s = _norm_matmul(xs, norm_a[j], w_ssd, bs, w_ssd.shape[1])
            conv_t = jnp.swapaxes(state_ssm_conv[j], 0, 1)
            y_ssd, h_new = _ssd_sample(proj, proj_s, conv_t, h_state, j, ssm_conv_w[j], ssm_conv_b[j], consts,
                                       ssm_norm[j], 8)
            qt, _, k_rot_t, _ = _rope(proj, tab_s, 1, bs, bs)
            y_diff = _diff_attn_sample(qt, k_rot_t[0], proj, cache_kt, cache_v4, n_pool, page_table, j, diff_lambda[j],
                                       diff_subln[j], lam_init)
            y_x = _xattn_sample(proj, xq_cb, mem_k_rows, mem_v_rows, layer, rb_x)
            xs = _out_proj([y_ssd, y_diff, y_x], w_out, xs, norm_f, final, bs)
            sk.append(k_rot_t[0].T.reshape(bs, 1, DIFF_HEADS, 2, DIFF_DK))
            sv.append(proj[:, v_lo:v_lo + DIFF_WIDTH].reshape(bs, 1, DIFF_HEADS, DIFF_DV))
            sssm.append(h_new.reshape(bs, SSD_GROUPS, SSD_HPG, SSD_HEADDIM, SSD_STATE))
            sconv.append(jnp.concatenate([state_ssm_conv[j][:, 1:], proj_s[:, None, :SSD_CONV_CH]], axis=1))
        else:
            w_in = w_in_c[j].astype(BF16)
            w_out = w_out_c[j].astype(BF16)
            xq_cb = 3 * CONF_WIDTH // X_WIDTH

            proj = _norm_matmul(xp, norm_c[j], w_in, tm_p, 1792)
            y_conf, tail = _conf_prompt(proj, conf_dw_w[j], conf_dw_b[j], conf_ln_g[j], conf_ln_b[j], bp, sp, tt_conf)
            y_x = _xattn_prompt(proj, xq_cb, mem_kv3, layer, bp, sp, tq_x)
            xp = _out_proj([y_conf, y_x], w_out, xp, norm_f, final, tm_out)
            pconf.append(tail[:, CONF_PAD - (CONF_CONV_W - 1):])

            proj = _norm_matmul(xs, norm_c[j], w_in, bs, 1024)
            st_t = jnp.swapaxes(state_conf_conv[j], 0, 1)
            y_conf, u = _conf_sample(proj, st_t, conf_dw_w[j], conf_dw_b[j], conf_ln_g[j], conf_ln_b[j], 256)
            y_x = _xattn_sample(proj, xq_cb, mem_k_rows, mem_v_rows, layer, rb_x)
            xs = _out_proj([y_conf, y_x], w_out, xs, norm_f, final, bs)
            sconf.append(jnp.concatenate([state_conf_conv[j][:, 1:], u[:, None]], axis=1))

    p_attn_k = jnp.transpose(jnp.stack(pk).reshape(n_a, bp, DIFF_HEADS, 2, DIFF_DK, sp), (0, 1, 5, 2, 3, 4))
    return (xp.reshape(bp, sp, d), xs.reshape(bs, 1, d), p_attn_k, jnp.stack(pv), jnp.stack(pssm),
            jnp.stack(pconv), jnp.stack(pconf), p_mem_k, p_mem_v, jnp.stack(sk), jnp.stack(sv), jnp.stack(sssm),
            jnp.stack(sconv), jnp.stack(sconf))
```

```python
import functools
import math

import jax
import jax.numpy as jnp
import numpy as np
from jax import lax
from jax.experimental import pallas as pl
from jax.experimental.pallas import tpu as pltpu

F32 = jnp.float32
BF16 = jnp.bfloat16

D_MODEL = 1024
DEPTH = 4
PAGE_SIZE = 128
SSD_INNER = 1024
SSD_HEADDIM = 64
SSD_HEADS = 16
SSD_GROUPS = 2
SSD_HPG = 8
SSD_STATE = 128
SSD_CONV = 4
SSD_CHUNK = 128
SSD_CONV_CH = 1536
DIFF_HEADS = 8
DIFF_DK = 64
DIFF_DV = 128
DIFF_QK = 1024
DIFF_WIDTH = 1024
ROT_DIM = 16
ROPE_THETA = 500000.0
CONF_WIDTH = 2048
CONF_CONV_W = 31
N_MEM = 256
X_HEADS = 4
X_HEADDIM = 128
X_WIDTH = 512
EPS = 1e-6

LANES = 128
SUBLANES = 8
VMEM_LIMIT_BYTES = 56 * 1024 * 1024
NEG_BIG = -1e30


def _params(*sem):
    return pltpu.CompilerParams(dimension_semantics=sem, vmem_limit_bytes=VMEM_LIMIT_BYTES)


def _silu(x):
    return x / (1.0 + jnp.exp(-x))


def _split2(x):
    hi = x.astype(BF16)
    lo = (x - hi.astype(F32)).astype(BF16)
    return hi, lo


def _split3(x):
    hi = x.astype(BF16)
    r = x - hi.astype(F32)
    mid = r.astype(BF16)
    lo = (r - mid.astype(F32)).astype(BF16)
    return hi, mid, lo


def _dot(a, b):
    return jnp.dot(a, b, preferred_element_type=F32)


def _dot_nt(a, b):
    return lax.dot_general(a, b, (((1,), (1,)), ((), ())), preferred_element_type=F32)


def _expand(parts, e):
    acc = _dot(parts[0], e)
    for p in parts[1:]:
        acc = acc + _dot(p, e)
    return acc


def _norm_matmul_body(x_ref, g_ref, w_ref, o_ref, h_ref):
    @pl.when(pl.program_id(1) == 0)
    def _():
        x = x_ref[...]
        r = lax.rsqrt(jnp.mean(x * x, axis=-1, keepdims=True) + EPS)
        h_ref[...] = ((x * r) * g_ref[...]).astype(BF16)

    o_ref[...] = _dot(h_ref[...], w_ref[...])


def _norm_matmul(x, g, w, tm, tn):
    m, d = x.shape
    n = w.shape[1]
    return pl.pallas_call(
        _norm_matmul_body,
        grid=(m // tm, n // tn),
        in_specs=[pl.BlockSpec((tm, d), lambda i, j: (i, 0)),
                  pl.BlockSpec((1, d), lambda i, j: (0, 0)),
                  pl.BlockSpec((d, tn), lambda i, j: (0, j))],
        out_specs=pl.BlockSpec((tm, tn), lambda i, j: (i, j)),
        out_shape=jax.ShapeDtypeStruct((m, n), F32),
        scratch_shapes=[pltpu.VMEM((tm, d), BF16)],
        compiler_params=_params("parallel", "arbitrary"),
        name="norm_matmul",
    )(x, g.reshape(1, d), w)


def _out_proj_body(*refs, n_in, final):
    ins = refs[:n_in]
    ws = refs[n_in:2 * n_in]
    x_ref = refs[2 * n_in]
    g_ref = refs[2 * n_in + 1]
    o_ref = refs[2 * n_in + 2]
    acc = x_ref[...]
    for a, w in zip(ins, ws):
        acc = acc + _dot(a[...], w[...])
    if final:
        r = lax.rsqrt(jnp.mean(acc * acc, axis=-1, keepdims=True) + EPS)
        acc = (acc * r) * g_ref[...]
    o_ref[...] = acc


def _out_proj(parts, w, x, norm_f, final, tm):
    m, d = x.shape
    widths = [p.shape[1] for p in parts]
    offs = np.cumsum([0] + widths).tolist()
    ws = [w[offs[i]:offs[i + 1]] for i in range(len(parts))]
    in_specs = ([pl.BlockSpec((tm, wd), lambda i: (i, 0)) for wd in widths]
                + [pl.BlockSpec((wd, d), lambda i: (0, 0)) for wd in widths]
                + [pl.BlockSpec((tm, d), lambda i: (i, 0)), pl.BlockSpec((1, d), lambda i: (0, 0))])
    return pl.pallas_call(
        functools.partial(_out_proj_body, n_in=len(parts), final=final),
        grid=(m // tm,),
        in_specs=in_specs,
        out_specs=pl.BlockSpec((tm, d), lambda i: (i, 0)),
        out_shape=jax.ShapeDtypeStruct((m, d), F32),
        compiler_params=_params("parallel"),
        name="out_proj",
    )(*parts, *ws, x, norm_f.reshape(1, d))


def _rope_tables(pos):
    half = ROT_DIM // 2
    inv_freq = ROPE_THETA ** (-jnp.arange(0, ROT_DIM, 2, dtype=F32) / ROT_DIM)
    ang = pos.astype(F32)[:, None] * inv_freq[None, :]
    cos, sin = jnp.cos(ang), jnp.sin(ang)
    n = pos.shape[0]
    zeros = jnp.zeros((n, DIFF_DK - ROT_DIM), F32)
    z8 = jnp.zeros((n, half), F32)
    c = jnp.concatenate([cos, cos, zeros + 1.0], axis=-1)
    sa = jnp.concatenate([-sin, z8, zeros], axis=-1)
    sb = jnp.concatenate([z8, sin, zeros], axis=-1)
    return (jnp.concatenate([c, c], axis=-1), jnp.concatenate([sa, sa], axis=-1),
            jnp.concatenate([sb, sb], axis=-1))


def _rope_body(q_ref, k_ref, v_ref, c_ref, sa_ref, sb_ref, qt_ref, kb_ref, kt_ref, vt_ref):
    c, sa, sb = c_ref[...], sa_ref[...], sb_ref[...]
    half = ROT_DIM // 2

    def rot(x):
        return x * c + pltpu.roll(x, LANES - half, 1) * sa + pltpu.roll(x, half, 1) * sb

    for h in range(DIFF_HEADS):
        sl = slice(h * LANES, (h + 1) * LANES)
        kr = rot(k_ref[:, sl])
        kb_ref[:, sl] = kr.astype(BF16)
        kt_ref[sl, :] = kr.T
        qt_ref[sl, :] = (rot(q_ref[:, sl]) * (DIFF_DK ** -0.5)).T.astype(BF16)
        vt_ref[sl, :] = v_ref[:, sl].T.astype(BF16)


def _rope(proj, tables, nb, s_len, tm):
    t = proj.shape[0]
    w = DIFF_QK
    nt = s_len // tm
    tab_spec = pl.BlockSpec((tm, LANES), lambda i: (i % nt, 0))
    row = lambda cb: pl.BlockSpec((tm, w), lambda i: (i, cb))
    col = pl.BlockSpec((w, tm), lambda i: (0, i))
    return pl.pallas_call(
        _rope_body,
        grid=(t // tm,),
        in_specs=[row(1), row(2), row(3), tab_spec, tab_spec, tab_spec],
        out_specs=[col, row(0), pl.BlockSpec((None, w, tm), lambda i: (i // nt, 0, i % nt)), col],
        out_shape=[jax.ShapeDtypeStruct((w, t), BF16), jax.ShapeDtypeStruct((t, w), BF16),
                   jax.ShapeDtypeStruct((nb, w, s_len), F32), jax.ShapeDtypeStruct((w, t), BF16)],
        compiler_params=_params("parallel"),
        name="rope",
    )(proj, proj, proj, *tables)


def _diff_lambda(lam_ref, lam_init):
    lv = lam_ref[...]
    a = jnp.sum(lv[0:1] * lv[1:2], axis=-1, keepdims=True)
    b = jnp.sum(lv[2:3] * lv[3:4], axis=-1, keepdims=True)
    return jnp.exp(a) - jnp.exp(b) + lam_init


def _subln_gate(o, sub_g, g, lam_init):
    r = lax.rsqrt(jnp.mean(o * o, axis=-1, keepdims=True) + EPS)
    return ((o * r) * sub_g) * (1.0 - lam_init) * _silu(g)


def _sublane_allreduce(x, op):
    for sh in (1, 2, 4):
        x = op(x, pltpu.roll(x, sh, 0))
    return x


def _diff_attn_body(qi_ref, ki_ref, qt_ref, k_ref, vt_ref, g_ref, lam_ref, sg_ref, o_ref,
                    m_ref, l_ref, acc_ref, *, t, lam_init):
    pidx = pl.program_id(1)
    qi = qi_ref[pidx]
    ki = ki_ref[pidx]
    ng = t // SUBLANES

    @pl.when(ki == 0)
    def _():
        m_ref[...] = jnp.full(m_ref.shape, NEG_BIG, F32)
        l_ref[...] = jnp.zeros(l_ref.shape, F32)
        acc_ref[...] = jnp.zeros(acc_ref.shape, F32)

    def head(h, masked):
        r0 = pl.multiple_of(h * LANES, LANES)
        qt = qt_ref[pl.ds(r0, LANES), :]
        k = k_ref[:, pl.ds(r0, LANES)]
        vt = vt_ref[pl.ds(r0, LANES), :]
        vt1 = jnp.concatenate([vt, jnp.ones((2 * SUBLANES, t), BF16)], axis=0)
        sub = lax.broadcasted_iota(jnp.int32, qt.shape, 0)
        zero = jnp.zeros_like(qt)
        s2 = [_dot(k, jnp.where(sub < DIFF_DK, qt, zero)), _dot(k, jnp.where(sub >= DIFF_DK, qt, zero))]
        for c in range(2):
            s = s2[c]
            if masked:
                kpos = lax.broadcasted_iota(jnp.int32, s.shape, 0)
                qpos = lax.broadcasted_iota(jnp.int32, s.shape, 1)
                s = jnp.where(kpos <= qpos, s, NEG_BIG)
            s3 = s.reshape(ng, SUBLANES, t)
            m_old = m_ref[h, c]
            m_new = jnp.maximum(m_old, _sublane_allreduce(jnp.max(s3, axis=0), jnp.maximum))
            alpha = jnp.exp(m_old - m_new)
            pb = jnp.exp((s3 - m_new[None]).reshape(t, t).astype(BF16))
            pv = _dot(vt1, pb)
            l_ref[h, c] = alpha * l_ref[h, c] + pv[DIFF_DV:DIFF_DV + SUBLANES]
            acc = acc_ref[h, c].reshape(DIFF_DV // SUBLANES, SUBLANES, t) * alpha[None]
            acc_ref[h, c] = acc.reshape(DIFF_DV, t) + pv[:DIFF_DV]
            m_ref[h, c] = m_new

    def finish(h):
        r0 = pl.multiple_of(h * LANES, LANES)
        lam = _diff_lambda(lam_ref, lam_init)
        parts = []
        for c in range(2):
            parts.append(acc_ref[h, c].reshape(DIFF_DV // SUBLANES, SUBLANES, t) / l_ref[h, c][None])
        o = (parts[0] - lam * parts[1]).reshape(DIFF_DV, t).T
        o_ref[:, pl.ds(r0, LANES)] = _subln_gate(o, sg_ref[...], g_ref[:, pl.ds(r0, LANES)], lam_init).astype(BF16)

    @pl.when(ki < qi)
    def _():
        def body(h, carry):
            head(h, False)
            return carry
        lax.fori_loop(0, DIFF_HEADS, body, 0, unroll=2)

    @pl.when(ki == qi)
    def _():
        def body(h, carry):
            head(h, True)
            finish(h)
            return carry
        lax.fori_loop(0, DIFF_HEADS, body, 0, unroll=2)


def _diff_attn_prompt(qt, kb, vt, proj, lam_vecs, sub_g, lam_init, nb, s_len, t):
    tt = kb.shape[0]
    nq = s_len // t
    pairs = [(i, j) for i in range(nq) for j in range(i + 1)]
    qi_tab = jnp.asarray([a for a, _ in pairs], jnp.int32)
    ki_tab = jnp.asarray([b for _, b in pairs], jnp.int32)
    g_cb = (SSD_INNER + 2 * DIFF_QK + DIFF_WIDTH) // DIFF_WIDTH
    w = DIFF_WIDTH
    grid_spec = pltpu.PrefetchScalarGridSpec(
        num_scalar_prefetch=2,
        grid=(nb, len(pairs)),
        in_specs=[pl.BlockSpec((w, t), lambda b, p, qtab, ktab: (0, b * nq + qtab[p])),
                  pl.BlockSpec((t, w), lambda b, p, qtab, ktab: (b * nq + ktab[p], 0)),
                  pl.BlockSpec((w, t), lambda b, p, qtab, ktab: (0, b * nq + ktab[p])),
                  pl.BlockSpec((t, w), lambda b, p, qtab, ktab: (b * nq + qtab[p], g_cb)),
                  pl.BlockSpec((4, DIFF_DK), lambda b, p, qtab, ktab: (0, 0)),
                  pl.BlockSpec((1, DIFF_DV), lambda b, p, qtab, ktab: (0, 0))],
        out_specs=pl.BlockSpec((t, w), lambda b, p, qtab, ktab: (b * nq + qtab[p], 0)),
        scratch_shapes=[pltpu.VMEM((DIFF_HEADS, 2, SUBLANES, t), F32), pltpu.VMEM((DIFF_HEADS, 2, SUBLANES, t), F32),
                        pltpu.VMEM((DIFF_HEADS, 2, DIFF_DV, t), F32)],
    )
    return pl.pallas_call(
        functools.partial(_diff_attn_body, t=t, lam_init=lam_init),
        grid_spec=grid_spec,
        out_shape=jax.ShapeDtypeStruct((tt, w), BF16),
        compiler_params=_params("parallel", "arbitrary"),
        name="diff_attn_prompt",
    )(qi_tab, ki_tab, qt, kb, vt, proj, lam_vecs, sub_g.reshape(1, DIFF_DV))


PAGES_PER_STEP = 16


def _diff_attn_sample_body(pt_ref, qt_ref, knt_ref, vn_ref, g_ref, *rest, lam_init):
    npg = PAGES_PER_STEP
    kt_refs = rest[:npg]
    v_refs = rest[npg:2 * npg]
    lam_ref, sg_ref, o_ref, qcol_ref, m_ref, l_ref, acc_ref = rest[2 * npg:]
    b = pl.program_id(0)
    p = pl.program_id(1)
    nreq = qt_ref.shape[1]
    shape4 = (DIFF_HEADS, 2, DIFF_DK, PAGE_SIZE)

    @pl.when(p == 0)
    def _():
        onehot = (lax.broadcasted_iota(jnp.int32, (nreq, LANES), 0) == b).astype(BF16)
        qcol = _dot(qt_ref[...], onehot)
        qcol_ref[...] = qcol
        kn_hi, kn_lo = _split2(knt_ref[...])
        kcol = _dot(kn_hi, onehot) + _dot(kn_lo, onehot)
        prod = (qcol * kcol).reshape(shape4)
        for c in range(2):
            m_ref[c] = jnp.sum(prod[:, c], axis=1)
            l_ref[c] = jnp.ones((DIFF_HEADS, LANES), F32)
            acc_ref[c] = vn_ref[...]

    lane = lax.broadcasted_iota(jnp.int32, (DIFF_HEADS, LANES), 1)
    sub_w = lax.broadcasted_iota(jnp.int32, (DIFF_HEADS, PAGE_SIZE * DIFF_HEADS), 0)
    lane_w = lax.broadcasted_iota(jnp.int32, (DIFF_HEADS, PAGE_SIZE * DIFF_HEADS), 1)
    own_head = sub_w == lane_w % DIFF_HEADS
    keys_per_block = LANES // DIFF_HEADS
    for i in range(npg):
        prod = (qcol_ref[...] * kt_refs[i][...]).reshape(shape4)
        v2 = v_refs[i][...].reshape(PAGE_SIZE * DIFF_HEADS, DIFF_DV).astype(BF16)
        ws, alphas = [], []
        for c in range(2):
            s = jnp.sum(prod[:, c], axis=1)
            m_old = m_ref[c]
            m_new = jnp.maximum(m_old, jnp.max(s, axis=-1, keepdims=True))
            alpha = jnp.exp(m_old - m_new)
            pr = jnp.exp(s - m_new)
            l_ref[c] = alpha * l_ref[c] + jnp.sum(pr, axis=-1, keepdims=True)
            m_ref[c] = m_new
            wexp = jnp.concatenate(
                [jnp.take_along_axis(pr, keys_per_block * j + lane // DIFF_HEADS, axis=1)
                 for j in range(DIFF_HEADS)], axis=1)
            ws.append(jnp.where(own_head, wexp, 0.0))
            alphas.append(alpha)
        pv = _dot(jnp.concatenate(ws, axis=0).astype(BF16), v2)
        for c in range(2):
            acc_ref[c] = alphas[c] * acc_ref[c] + pv[c * DIFF_HEADS:(c + 1) * DIFF_HEADS]

    @pl.when(p == pl.num_programs(1) - 1)
    def _():
        lam = _diff_lambda(lam_ref, lam_init)
        o = acc_ref[0] / l_ref[0] - lam * (acc_ref[1] / l_ref[1])
        o_ref[...] = _subln_gate(o, sg_ref[...], g_ref[...], lam_init).astype(BF16)


def _diff_attn_sample(qt, kn_t, proj, cache_kt, cache_v, n_pool, page_table, layer_a, lam_vecs, sub_g, lam_init):
    nreq = kn_t.shape[1]
    n_pages = page_table.shape[1]
    npg = PAGES_PER_STEP
    assert n_pages % npg == 0
    base = layer_a * n_pool
    proj3 = proj.reshape(nreq, proj.shape[1] // DIFF_DV, DIFF_DV)
    v_cb = (SSD_INNER + 2 * DIFF_QK) // DIFF_WIDTH
    head_rows = lambda cb: pl.BlockSpec((None, DIFF_HEADS, DIFF_DV), lambda b, p, pt: (b, cb, 0))
    const = lambda shape: pl.BlockSpec(shape, lambda b, p, pt: (0,) * len(shape))

    def page_spec(shape, i):
        zeros = (0,) * (len(shape) - 1)
        return pl.BlockSpec(shape, lambda b, p, pt: (base + pt[b * n_pages + p * npg + i],) + zeros)

    grid_spec = pltpu.PrefetchScalarGridSpec(
        num_scalar_prefetch=1,
        grid=(nreq, n_pages // npg),
        in_specs=([const((DIFF_QK, nreq)), const((DIFF_QK, nreq)), head_rows(v_cb), head_rows(v_cb + 1)]
                  + [page_spec((None, DIFF_QK, PAGE_SIZE), i) for i in range(npg)]
                  + [page_spec((None, PAGE_SIZE, DIFF_HEADS, DIFF_DV), i) for i in range(npg)]
                  + [const((4, DIFF_DK)), const((1, DIFF_DV))]),
        out_specs=head_rows(0),
        scratch_shapes=[pltpu.VMEM((DIFF_QK, LANES), F32), pltpu.VMEM((2, DIFF_HEADS, LANES), F32),
                        pltpu.VMEM((2, DIFF_HEADS, LANES), F32), pltpu.VMEM((2, DIFF_HEADS, DIFF_DV), F32)],
    )
    out = pl.pallas_call(
        functools.partial(_diff_attn_sample_body, lam_init=lam_init),
        grid_spec=grid_spec,
        out_shape=jax.ShapeDtypeStruct((nreq, DIFF_HEADS, DIFF_DV), BF16),
        compiler_params=_params("parallel", "arbitrary"),
        name="diff_attn_sample",
    )(page_table.reshape(-1), qt, kn_t, proj3, proj3, *([cache_kt] * npg), *([cache_v] * npg), lam_vecs,
      sub_g.reshape(1, DIFF_DV))
    return out.reshape(nreq, DIFF_WIDTH)


def _xattn_heads(xq, xg, mk, mv):
    outs = []
    for h in range(X_HEADS):
        sl = slice(h * X_HEADDIM, (h + 1) * X_HEADDIM)
        s = _dot_nt(xq[:, sl].astype(BF16), mk[:, sl].astype(BF16)) * (X_HEADDIM ** -0.5)
        e = jnp.exp(s - jnp.max(s, axis=-1, keepdims=True))
        pr = (e / jnp.sum(e, axis=-1, keepdims=True)).astype(BF16)
        outs.append(_dot(pr, mv[:, sl].astype(BF16)) * _silu(xg[:, sl]))
    return outs


def _xattn_prompt_body(xq_ref, xg_ref, mk_ref, mv_ref, o_ref):
    outs = _xattn_heads(xq_ref[...], xg_ref[...], mk_ref[...], mv_ref[...])
    for h in range(X_HEADS):
        o_ref[:, h * X_HEADDIM:(h + 1) * X_HEADDIM] = outs[h].astype(BF16)


def _xattn_prompt(proj, xq_cb, mem_kv, layer, nb, s_len, tq):
    t = proj.shape[0]
    nq = s_len // tq
    return pl.pallas_call(
        _xattn_prompt_body,
        grid=(nb, nq),
        in_specs=[pl.BlockSpec((tq, X_WIDTH), lambda b, i: (b * nq + i, xq_cb)),
                  pl.BlockSpec((tq, X_WIDTH), lambda b, i: (b * nq + i, xq_cb + 1)),
                  pl.BlockSpec((None, N_MEM, X_WIDTH), lambda b, i: (b, 0, 2 * layer)),
                  pl.BlockSpec((None, N_MEM, X_WIDTH), lambda b, i: (b, 0, 2 * layer + 1))],
        out_specs=pl.BlockSpec((tq, X_WIDTH), lambda b, i: (b * nq + i, 0)),
        out_shape=jax.ShapeDtypeStruct((t, X_WIDTH), BF16),
        compiler_params=_params("parallel", "parallel"),
        name="xattn_prompt",
    )(proj, proj, mem_kv, mem_kv)


def _xattn_sample_body(xqg_ref, mk_ref, mv_ref, o_ref, *, rb):
    shape = (2 * X_HEADS, N_MEM * X_HEADS)
    own_head = lax.broadcasted_iota(jnp.int32, shape, 0) == lax.broadcasted_iota(jnp.int32, shape, 1) % X_HEADS
    scores = []
    for r in range(rb):
        s = _dot_nt(xqg_ref[r].astype(BF16), mk_ref[r].astype(BF16)) * (X_HEADDIM ** -0.5)
        scores.append(jnp.where(own_head, s, NEG_BIG))
    probs = []
    for s in scores:
        e = jnp.exp(s - jnp.max(s, axis=-1, keepdims=True))
        probs.append((e / jnp.sum(e, axis=-1, keepdims=True)).astype(BF16))
    for r in range(rb):
        o = _dot(probs[r], mv_ref[r].astype(BF16))[0:X_HEADS]
        o_ref[r] = (o * _silu(xqg_ref[r, X_HEADS:2 * X_HEADS, :])).astype(BF16)


def _xattn_sample(proj, xq_cb, mem_k, mem_v, layer, rb):
    nreq = proj.shape[0]
    p3 = proj.reshape(nreq, proj.shape[1] // X_HEADDIM, X_HEADDIM)
    mem_spec = pl.BlockSpec((None, rb, N_MEM * X_HEADS, X_HEADDIM), lambda i: (layer, i, 0, 0))
    out = pl.pallas_call(
        functools.partial(_xattn_sample_body, rb=rb),
        grid=(nreq // rb,),
        in_specs=[pl.BlockSpec((rb, 2 * X_HEADS, X_HEADDIM), lambda i: (i, xq_cb // 2, 0)), mem_spec, mem_spec],
        out_specs=pl.BlockSpec((rb, X_HEADS, X_HEADDIM), lambda i: (i, 0, 0)),
        out_shape=jax.ShapeDtypeStruct((nreq, X_HEADS, X_HEADDIM), BF16),
        compiler_params=_params("parallel"),
        name="xattn_sample",
    )(p3, mem_k, mem_v)
    return out.reshape(nreq, X_WIDTH)


def _ssd_consts(dt_bias, a_log, d_skip):
    pad = LANES - SSD_HEADS
    dtb = jnp.pad(dt_bias.astype(F32), (0, pad)).reshape(1, LANES)
    alog = jnp.pad(a_log.astype(F32), (0, pad)).reshape(1, LANES)
    dexp = jnp.repeat(d_skip.astype(F32), SSD_HEADDIM).reshape(1, SSD_INNER)
    head = np.arange(LANES)[:, None]
    chan_head = np.arange(SSD_INNER)[None, :] // SSD_HEADDIM
    eexp = jnp.asarray((head == chan_head).astype(np.float32), BF16)
    return dtb, alog, dexp, eexp


def _softplus(x):
    return jnp.maximum(x, 0.0) + jnp.log1p(jnp.exp(-jnp.abs(x)))


def _ssd_gate_norm(y, xs, z, dexp, ng):
    y = (y + dexp * xs) * _silu(z)
    r = lax.rsqrt(jnp.mean(y * y, axis=-1, keepdims=True) + EPS)
    return ((y * r) * ng).astype(BF16)


def _ssd_prompt_body(z_ref, xbc_ref, dtr_ref, cw_ref, cb_ref, dtb_ref, alog_ref, dexp_ref, ng_ref, eexp_ref,
                     y_ref, hout_ref, ext_ref, h_ref, yd_ref):
    q = SSD_CHUNK
    c = pl.program_id(1)

    @pl.when(c == 0)
    def _():
        ext_ref[0:SUBLANES, :] = jnp.zeros((SUBLANES, SSD_CONV_CH), F32)
        h_ref[...] = jnp.zeros(h_ref.shape, F32)

    x_raw = xbc_ref[...]
    ext_ref[SUBLANES:SUBLANES + q, :] = x_raw
    conv = cb_ref[...] + cw_ref[SSD_CONV - 1:SSD_CONV, :] * x_raw
    for k in range(1, SSD_CONV):
        conv = conv + cw_ref[SSD_CONV - 1 - k:SSD_CONV - k, :] * ext_ref[SUBLANES - k:SUBLANES - k + q, :]
    ext_ref[0:SUBLANES, :] = x_raw[q - SUBLANES:q, :]
    xbc = _silu(conv)
    xs = xbc[:, :SSD_INNER]
    bm = xbc[:, SSD_INNER:SSD_INNER + SSD_GROUPS * SSD_STATE].astype(BF16)
    cm = xbc[:, SSD_INNER + SSD_GROUPS * SSD_STATE:].astype(BF16)

    eexp = eexp_ref[...]
    dt = _softplus(dtr_ref[...] + dtb_ref[...])
    da = dt * (-jnp.exp(alog_ref[...]))
    ri = lax.broadcasted_iota(jnp.int32, (q, q), 0)
    ci = lax.broadcasted_iota(jnp.int32, (q, q), 1)
    tri = ri >= ci
    tri_b = tri.astype(BF16)
    hi, mid, lo = _split3(da)
    acum = _dot(tri_b, hi) + _dot(tri_b, mid) + _dot(tri_b, lo)
    acum_t = acum.T
    a_last = acum[q - 1:q, :]
    xdt = xs * _expand(_split2(dt), eexp)
    xdt_b = xdt.astype(BF16)
    ea_exp = _expand(_split2(jnp.exp(acum)), eexp)
    wend_exp = _expand(_split2(jnp.exp(a_last - acum)), eexp)
    xw_t = (xdt * wend_exp).T.astype(BF16)
    decay = jnp.exp(a_last)

    lane = lax.broadcasted_iota(jnp.int32, (q, LANES), 1)
    for g in range(SSD_GROUPS):
        gs = slice(g * SSD_STATE, (g + 1) * SSD_STATE)
        ch = slice(g * SSD_HPG * SSD_HEADDIM, (g + 1) * SSD_HPG * SSD_HEADDIM)
        scores = _dot_nt(cm[:, gs], bm[:, gs])
        for pair in range(SSD_HPG // 2):
            halves = []
            for r in range(2):
                hd = g * SSD_HPG + 2 * pair + r
                seg = acum[:, hd:hd + 1] - acum_t[hd:hd + 1, :]
                ldec = jnp.where(tri, jnp.exp(jnp.where(tri, seg, 0.0)), 0.0)
                xp = xdt_b[:, (hd // 2) * LANES:(hd // 2 + 1) * LANES]
                halves.append(_dot((scores * ldec).astype(BF16), xp))
            lo_lane = (g * SSD_HPG + 2 * pair) // 2 * LANES
            yd_ref[:, lo_lane:lo_lane + LANES] = jnp.where(lane < SSD_HEADDIM, halves[0], halves[1])
        h_g = h_ref[ch, :]
        yd_ref[:, ch] = yd_ref[:, ch] + _dot_nt(cm[:, gs], h_g.astype(BF16)) * ea_exp[:, ch]
        upd = _dot(xw_t[ch, :], bm[:, gs])
        for r in range(SSD_HPG):
            hd = g * SSD_HPG + r
            rows = slice(hd * SSD_HEADDIM, (hd + 1) * SSD_HEADDIM)
            h_ref[rows, :] = h_ref[rows, :] * decay[:, hd:hd + 1] + upd[r * SSD_HEADDIM:(r + 1) * SSD_HEADDIM, :]

    y_ref[...] = _ssd_gate_norm(yd_ref[...], xs, z_ref[...], dexp_ref[...], ng_ref[...])
    hout_ref[...] = h_ref[...]


def _ssd_prompt(proj, proj_s, conv_w, conv_b, consts, norm_g, nb, s_len):
    t = proj.shape[0]
    q = SSD_CHUNK
    nc = s_len // q
    dtb, alog, dexp, eexp = consts
    const = lambda shape: pl.BlockSpec(shape, lambda b, c: (0,) * len(shape))
    return pl.pallas_call(
        _ssd_prompt_body,
        grid=(nb, nc),
        in_specs=[pl.BlockSpec((q, SSD_INNER), lambda b, c: (b * nc + c, 0)),
                  pl.BlockSpec((q, SSD_CONV_CH), lambda b, c: (b * nc + c, 0)),
                  pl.BlockSpec((q, LANES), lambda b, c: (b * nc + c, SSD_CONV_CH // LANES)),
                  const((SSD_CONV, SSD_CONV_CH)), const((1, SSD_CONV_CH)), const((1, LANES)), const((1, LANES)),
                  const((1, SSD_INNER)), const((1, SSD_INNER)), const((LANES, SSD_INNER))],
        out_specs=[pl.BlockSpec((q, SSD_INNER), lambda b, c: (b * nc + c, 0)),
                   pl.BlockSpec((None, SSD_INNER, SSD_STATE), lambda b, c: (b, 0, 0))],
        out_shape=[jax.ShapeDtypeStruct((t, SSD_INNER), BF16),
                   jax.ShapeDtypeStruct((nb, SSD_INNER, SSD_STATE), F32)],
        scratch_shapes=[pltpu.VMEM((SUBLANES + q, SSD_CONV_CH), F32), pltpu.VMEM((SSD_INNER, SSD_STATE), F32),
                        pltpu.VMEM((q, SSD_INNER), F32)],
        compiler_params=_params("parallel", "arbitrary"),
        name="ssd_prompt",
    )(proj, proj_s, proj_s, conv_w, conv_b.reshape(1, -1), dtb, alog, dexp, norm_g.reshape(1, -1), eexp)


def _ssd_sample_body(z_ref, xbc_ref, dtr_ref, cs_ref, hin_ref, cw_ref, cb_ref, dtb_ref, alog_ref, dexp_ref, ng_ref,
                     eexp_ref, y_ref, hout_ref, xs_ref, xdt_t_ref, da_hi_ref, da_lo_ref, bm_ref, cm_t_ref, y_t_ref,
                     *, rb):
    i = pl.program_id(0)
    nreq = z_ref.shape[0]

    @pl.when(i == 0)
    def _():
        conv = cb_ref[...] + cw_ref[SSD_CONV - 1:SSD_CONV, :] * xbc_ref[...]
        for k in range(SSD_CONV - 1):
            conv = conv + cw_ref[k:k + 1, :] * cs_ref[k]
        xbc = _silu(conv)
        xs = xbc[:, :SSD_INNER]
        xs_ref[...] = xs
        bm_ref[...] = xbc[:, SSD_INNER:SSD_INNER + SSD_GROUPS * SSD_STATE]
        cm_t_ref[...] = xbc[:, SSD_INNER + SSD_GROUPS * SSD_STATE:].T
        dt = _softplus(dtr_ref[...] + dtb_ref[...])
        da_hi, da_lo = _split2(jnp.exp(dt * (-jnp.exp(alog_ref[...]))).T)
        da_hi_ref[...] = da_hi
        da_lo_ref[...] = da_lo
        xdt_t_ref[...] = (xs * _expand(_split2(dt), eexp_ref[...])).T.astype(BF16)
        y_t_ref[...] = jnp.zeros(y_t_ref.shape, F32)

    def per_request(r, carry):
        b = i * rb + r
        onehot = (lax.broadcasted_iota(jnp.int32, (nreq, LANES), 0) == b).astype(BF16)
        xb = _dot(xdt_t_ref[...], onehot)
        dab = _dot(da_hi_ref[...], onehot) + _dot(da_lo_ref[...], onehot)
        brow = bm_ref[pl.ds(b, 1), :]
        cmask = lax.broadcasted_iota(jnp.int32, (SSD_STATE, nreq), 1) == b
        for g in range(SSD_GROUPS):
            gs = slice(g * SSD_STATE, (g + 1) * SSD_STATE)
            ch = slice(g * SSD_HPG * SSD_HEADDIM, (g + 1) * SSD_HPG * SSD_HEADDIM)
            for hr in range(SSD_HPG):
                hd = g * SSD_HPG + hr
                rows = slice(hd * SSD_HEADDIM, (hd + 1) * SSD_HEADDIM)
                hout_ref[r, rows, :] = dab[hd:hd + 1, :] * hin_ref[r, rows, :] + xb[rows, :] * brow[:, gs]
            cmat = jnp.where(cmask, cm_t_ref[gs, :], 0.0).astype(BF16)
            y_t_ref[ch, :] = y_t_ref[ch, :] + _dot(hout_ref[r, ch, :].astype(BF16), cmat)
        return carry

    lax.fori_loop(0, rb, per_request, 0)

    @pl.when(i == pl.num_programs(0) - 1)
    def _():
        y_ref[...] = _ssd_gate_norm(y_t_ref[...].T, xs_ref[...], z_ref[...], dexp_ref[...], ng_ref[...])


def _ssd_sample(proj, proj_s, conv_state_t, h_state, layer_a, conv_w, conv_b, consts, norm_g, rb):
    nreq = proj.shape[0]
    dtb, alog, dexp, eexp = consts
    const = lambda shape: pl.BlockSpec(shape, lambda i: (0,) * len(shape))
    return pl.pallas_call(
        functools.partial(_ssd_sample_body, rb=rb),
        grid=(nreq // rb,),
        in_specs=[pl.BlockSpec((nreq, SSD_INNER), lambda i: (0, 0)),
                  pl.BlockSpec((nreq, SSD_CONV_CH), lambda i: (0, 0)),
                  pl.BlockSpec((nreq, LANES), lambda i: (0, SSD_CONV_CH // LANES)),
                  const((SSD_CONV - 1, nreq, SSD_CONV_CH)),
                  pl.BlockSpec((None, rb, SSD_INNER, SSD_STATE), lambda i: (layer_a, i, 0, 0)),
                  const((SSD_CONV, SSD_CONV_CH)), const((1, SSD_CONV_CH)), const((1, LANES)), const((1, LANES)),
                  const((1, SSD_INNER)), const((1, SSD_INNER)), const((LANES, SSD_INNER))],
        out_specs=[const((nreq, SSD_INNER)),
                   pl.BlockSpec((rb, SSD_INNER, SSD_STATE), lambda i: (i, 0, 0))],
        out_shape=[jax.ShapeDtypeStruct((nreq, SSD_INNER), BF16),
                   jax.ShapeDtypeStruct(h_state.shape[1:], F32)],
        scratch_shapes=[pltpu.VMEM((nreq, SSD_INNER), F32), pltpu.VMEM((SSD_INNER, nreq), BF16),
                        pltpu.VMEM((LANES, nreq), BF16), pltpu.VMEM((LANES, nreq), BF16),
                        pltpu.VMEM((nreq, SSD_GROUPS * SSD_STATE), F32),
                        pltpu.VMEM((SSD_GROUPS * SSD_STATE, nreq), F32), pltpu.VMEM((SSD_INNER, nreq), F32)],
        compiler_params=_params("arbitrary"),
        name="ssd_sample",
    )(proj, proj_s, proj_s, conv_state_t, h_state, conv_w, conv_b.reshape(1, -1), dtb, alog, dexp,
      norm_g.reshape(1, -1), eexp)


CONF_PAD = 32
CONF_RB = 128
CONF_CB = LANES
CONF_NCB = CONF_WIDTH // CONF_CB


def _conf_finish(c, g, ln_g, ln_b):
    mu = jnp.mean(c, axis=-1, keepdims=True)
    xc = c - mu
    r = lax.rsqrt(jnp.mean(xc * xc, axis=-1, keepdims=True) + EPS)
    return (_silu((xc * r) * ln_g + ln_b) * _silu(g)).astype(BF16)


def _conf_prompt_body(a_ref, ag_ref, g_ref, w_ref, b_ref, lg_ref, lb_ref, y_ref, tail_ref, sh_ref, c_ref, *, tt):
    i = pl.program_id(1)
    rows = CONF_PAD + tt

    @pl.when(i == 0)
    def _():
        sh_ref[0, :, 0:CONF_PAD, :] = jnp.zeros((CONF_NCB, CONF_PAD, CONF_CB), F32)

    def glu(cb, carry):
        c0 = pl.multiple_of(cb * CONF_CB, CONF_CB)
        u = a_ref[:, pl.ds(c0, CONF_CB)] / (1.0 + jnp.exp(-ag_ref[:, pl.ds(c0, CONF_CB)]))
        tail_ref[:, pl.ds(c0, CONF_CB)] = u[tt - CONF_PAD:tt, :]
        sh_ref[0, cb, CONF_PAD:rows, :] = u
        for s in range(1, SUBLANES):
            sh_ref[s, cb, 0:rows - SUBLANES, :] = sh_ref[0, cb, s:s + rows - SUBLANES, :]
        return carry

    lax.fori_loop(0, CONF_NCB, glu, 0)
    first = CONF_PAD - (CONF_CONV_W - 1)

    def tile(idx, carry):
        rb = idx // CONF_NCB
        cb = idx % CONF_NCB
        r0 = pl.multiple_of(rb * CONF_RB, CONF_RB)
        acc = jnp.broadcast_to(b_ref[cb], (CONF_RB, CONF_CB))
        for k in range(CONF_CONV_W):
            off = first + k
            src = sh_ref[off % SUBLANES, cb, pl.ds(r0 + (off // SUBLANES) * SUBLANES, CONF_RB), :]
            acc = acc + w_ref[k, cb] * src
        c_ref[cb, pl.ds(r0, CONF_RB), :] = acc
        return carry

    lax.fori_loop(0, (tt // CONF_RB) * CONF_NCB, tile, 0)

    def lane_mean(x):
        return jnp.broadcast_to(jnp.sum(x, axis=-1, keepdims=True) * (1.0 / CONF_WIDTH), x.shape)

    zero = jnp.zeros((tt, CONF_CB), F32)
    mu = lane_mean(lax.fori_loop(0, CONF_NCB, lambda cb, acc: acc + c_ref[cb], zero))
    var = lane_mean(lax.fori_loop(0, CONF_NCB, lambda cb, acc: acc + (c_ref[cb] - mu) * (c_ref[cb] - mu), zero))
    r = lax.rsqrt(var + EPS)

    def finish(cb, carry):
        c0 = pl.multiple_of(cb * CONF_CB, CONF_CB)
        sh_ref[0, cb, 0:CONF_PAD, :] = sh_ref[0, cb, tt:rows, :]
        y = _silu(((c_ref[cb] - mu) * r) * lg_ref[cb] + lb_ref[cb]) * _silu(g_ref[:, pl.ds(c0, CONF_CB)])
        y_ref[:, pl.ds(c0, CONF_CB)] = y.astype(BF16)
        return carry

    lax.fori_loop(0, CONF_NCB, finish, 0)


def _conf_prompt(proj, dw_w, dw_b, ln_g, ln_b, nb, s_len, tt):
    t = proj.shape[0]
    nt = s_len // tt
    col = lambda cb: pl.BlockSpec((tt, CONF_WIDTH), lambda b, i: (b * nt + i, cb))
    const = lambda shape: pl.BlockSpec(shape, lambda b, i: (0,) * len(shape))
    return pl.pallas_call(
        functools.partial(_conf_prompt_body, tt=tt),
        grid=(nb, nt),
        in_specs=[col(0), col(1), col(2), const((CONF_CONV_W, CONF_NCB, 1, CONF_CB)), const((CONF_NCB, 1, CONF_CB)),
                  const((CONF_NCB, 1, CONF_CB)), const((CONF_NCB, 1, CONF_CB))],
        out_specs=[pl.BlockSpec((tt, CONF_WIDTH), lambda b, i: (b * nt + i, 0)),
                   pl.BlockSpec((None, CONF_PAD, CONF_WIDTH), lambda b, i: (b, 0, 0))],
        out_shape=[jax.ShapeDtypeStruct((t, CONF_WIDTH), BF16),
                   jax.ShapeDtypeStruct((nb, CONF_PAD, CONF_WIDTH), F32)],
        scratch_shapes=[pltpu.VMEM((SUBLANES, CONF_NCB, CONF_PAD + tt, CONF_CB), F32),
                        pltpu.VMEM((CONF_NCB, tt, CONF_CB), F32)],
        compiler_params=_params("parallel", "arbitrary"),
        name="conf_prompt",
    )(proj, proj, proj, dw_w.reshape(CONF_CONV_W, CONF_NCB, 1, CONF_CB), dw_b.reshape(CONF_NCB, 1, CONF_CB),
      ln_g.reshape(CONF_NCB, 1, CONF_CB), ln_b.reshape(CONF_NCB, 1, CONF_CB))


def _conf_sample_body(a_ref, ag_ref, g_ref, st_ref, w_ref, b_ref, lg_ref, lb_ref, y_ref, u_ref, c_ref, *, tc):
    j = pl.program_id(0)
    u = a_ref[...] / (1.0 + jnp.exp(-ag_ref[...]))
    u_ref[...] = u
    acc = b_ref[...] + w_ref[CONF_CONV_W - 1:CONF_CONV_W, :] * u
    for k in range(CONF_CONV_W - 1):
        acc = acc + w_ref[k:k + 1, :] * st_ref[k]
    c_ref[j] = acc

    @pl.when(j == pl.num_programs(0) - 1)
    def _():
        c = jnp.concatenate([c_ref[k] for k in range(CONF_WIDTH // tc)], axis=1)
        y_ref[...] = _conf_finish(c, g_ref[...], lg_ref[...], lb_ref[...])


def _conf_sample(proj, state_t, dw_w, dw_b, ln_g, ln_b, tc):
    nreq = proj.shape[0]
    nblk = CONF_WIDTH // tc
    col = lambda off: pl.BlockSpec((nreq, tc), lambda j: (0, off * nblk + j))
    vec = pl.BlockSpec((1, tc), lambda j: (0, j))
    full = pl.BlockSpec((1, CONF_WIDTH), lambda j: (0, 0))
    return pl.pallas_call(
        functools.partial(_conf_sample_body, tc=tc),
        grid=(nblk,),
        in_specs=[col(0), col(1), pl.BlockSpec((nreq, CONF_WIDTH), lambda j: (0, 2)),
                  pl.BlockSpec((CONF_CONV_W - 1, nreq, tc), lambda j: (0, 0, j)),
                  pl.BlockSpec((CONF_CONV_W, tc), lambda j: (0, j)), vec, full, full],
        out_specs=[pl.BlockSpec((nreq, CONF_WIDTH), lambda j: (0, 0)),
                   pl.BlockSpec((nreq, tc), lambda j: (0, j))],
        out_shape=[jax.ShapeDtypeStruct((nreq, CONF_WIDTH), BF16),
                   jax.ShapeDtypeStruct((nreq, CONF_WIDTH), F32)],
        scratch_shapes=[pltpu.VMEM((nblk, nreq, tc), F32)],
        compiler_params=_params("arbitrary"),
        name="conf_sample",
    )(proj, proj, proj, state_t, dw_w, dw_b.reshape(1, -1), ln_g.reshape(1, -1), ln_b.reshape(1, -1))


def _even_weights(w_in):
    cuts = np.cumsum([SSD_INNER, SSD_CONV_CH, SSD_HEADS]).tolist()
    z, xbc, dt, rest = (w_in[:, :cuts[0]], w_in[:, cuts[0]:cuts[1]], w_in[:, cuts[1]:cuts[2]], w_in[:, cuts[2]:])
    main = jnp.concatenate([z, rest], axis=1).astype(BF16)
    pad = jnp.zeros((w_in.shape[0], LANES - SSD_HEADS), w_in.dtype)
    ssd = jnp.concatenate([xbc, dt, pad], axis=1).astype(BF16)
    return main, ssd


def kernel(x_prompt, x_sample, cache_attn_k, cache_attn_v, cache_mem_k, cache_mem_v, state_ssm, state_ssm_conv,
           state_conf_conv, page_table, mem_prompt, norm_a, w_in_a, ssm_conv_w, ssm_conv_b, ssm_dt_bias, ssm_a_log,
           ssm_d, ssm_norm, diff_lambda, diff_subln, w_out_a, norm_c, w_in_c, conf_dw_w, conf_dw_b, conf_ln_g,
           conf_ln_b, w_out_c, xattn_wk, xattn_wv, mem_norm, norm_f):
    bp, sp, d = x_prompt.shape
    bs = x_sample.shape[0]
    n_a = w_in_a.shape[0]
    n_pool = cache_attn_k.shape[1]
    past = page_table.shape[1] * PAGE_SIZE
    tp = bp * sp

    tm_p = min(1024, sp)
    t_attn = min(1024, sp)
    tm_rope = min(512, sp)
    rb_x = 8
    tq_x = min(512, sp)
    tt_conf = min(256, sp)
    tm_out = min(512, sp)

    xp = x_prompt.reshape(tp, d)
    xs = x_sample.reshape(bs, d)

    w_mem = jnp.concatenate([jnp.concatenate([xattn_wk[l], xattn_wv[l]], axis=1) for l in range(DEPTH)],
                            axis=1).astype(BF16)
    mem_kv = _norm_matmul(mem_prompt.reshape(bp * N_MEM, d), mem_norm, w_mem, N_MEM, 1024)
    mem_kv3 = mem_kv.reshape(bp, N_MEM, DEPTH * 2 * X_WIDTH)
    mem_kv5 = mem_kv.reshape(bp, N_MEM, DEPTH, 2, X_HEADS, X_HEADDIM)
    p_mem_k = jnp.moveaxis(mem_kv5[:, :, :, 0], 2, 0)
    p_mem_v = jnp.moveaxis(mem_kv5[:, :, :, 1], 2, 0)

    tab_p = _rope_tables(jnp.arange(sp))
    tab_s = _rope_tables(jnp.full((bs,), past, jnp.int32))
    cache_kt = jnp.swapaxes(cache_attn_k.reshape(n_a * n_pool, PAGE_SIZE, DIFF_QK), 1, 2)
    cache_v4 = cache_attn_v.reshape(n_a * n_pool, PAGE_SIZE, DIFF_HEADS, DIFF_DV)
    h_state = state_ssm.reshape(n_a, bs, SSD_INNER, SSD_STATE)
    mem_k_rows = cache_mem_k.reshape(DEPTH, bs, N_MEM * X_HEADS, X_HEADDIM)
    mem_v_rows = cache_mem_v.reshape(DEPTH, bs, N_MEM * X_HEADS, X_HEADDIM)

    pk, pv, pssm, pconv, pconf = [], [], [], [], []
    sk, sv, sssm, sconv, sconf = [], [], [], [], []
    for layer in range(DEPTH):
        j = layer // 2
        final = layer == DEPTH - 1
        if layer % 2 == 0:
            lam_init = 0.8 - 0.6 * math.exp(-0.3 * layer)
            w_main, w_ssd = _even_weights(w_in_a[j])
            w_out = w_out_a[j].astype(BF16)
            consts = _ssd_consts(ssm_dt_bias[j], ssm_a_log[j], ssm_d[j])
            xq_cb = (SSD_INNER + 2 * DIFF_QK + 2 * DIFF_WIDTH) // X_WIDTH
            v_lo = SSD_INNER + 2 * DIFF_QK

            proj = _norm_matmul(xp, norm_a[j], w_main, tm_p, 3072)
            proj_s = _norm_matmul(xp, norm_a[j], w_ssd, tm_p, w_ssd.shape[1])
            y_ssd, h_last = _ssd_prompt(proj, proj_s, ssm_conv_w[j], ssm_conv_b[j], consts, ssm_norm[j], bp, sp)
            qt, kb, k_rot_t, vt = _rope(proj, tab_p, bp, sp, tm_rope)
            y_diff = _diff_attn_prompt(qt, kb, vt, proj, diff_lambda[j], diff_subln[j], lam_init, bp, sp, t_attn)
            y_x = _xattn_prompt(proj, xq_cb, mem_kv3, layer, bp, sp, tq_x)
            xp = _out_proj([y_ssd, y_diff, y_x], w_out, xp, norm_f, final, tm_out)
            pk.append(k_rot_t)
            pv.append(proj[:, v_lo:v_lo + DIFF_WIDTH].reshape(bp, sp, DIFF_HEADS, DIFF_DV))
            pssm.append(h_last.reshape(bp, SSD_GROUPS, SSD_HPG, SSD_HEADDIM, SSD_STATE))
            pconv.append(proj_s.reshape(bp, sp, -1)[:, sp - (SSD_CONV - 1):, :SSD_CONV_CH])

            proj = _norm_matmul(xs, norm_a[j], w_main, bs, 1024)
            proj_s = _norm_matmul(xs, norm_a[j], w_ssd, bs, w_ssd.shape[1])
            conv_t = jnp.swapaxes(state_ssm_conv[j], 0, 1)
            y_ssd, h_new = _ssd_sample(proj, proj_s, conv_t, h_state, j, ssm_conv_w[j], ssm_conv_b[j], consts,
                                       ssm_norm[j], 8)
            qt, _, k_rot_t, _ = _rope(proj, tab_s, 1, bs, bs)
            y_diff = _diff_attn_sample(qt, k_rot_t[0], proj, cache_kt, cache_v4, n_pool, page_table, j, diff_lambda[j],
                                       diff_subln[j], lam_init)
            y_x = _xattn_sample(proj, xq_cb, mem_k_rows, mem_v_rows, layer, rb_x)
            xs = _out_proj([y_ssd, y_diff, y_x], w_out, xs, norm_f, final, bs)
            sk.append(k_rot_t[0].T.reshape(bs, 1, DIFF_HEADS, 2, DIFF_DK))
            sv.append(proj[:, v_lo:v_lo + DIFF_WIDTH].reshape(bs, 1, DIFF_HEADS, DIFF_DV))
            sssm.append(h_new.reshape(bs, SSD_GROUPS, SSD_HPG, SSD_HEADDIM, SSD_STATE))
            sconv.append(jnp.concatenate([state_ssm_conv[j][:, 1:], proj_s[:, None, :SSD_CONV_CH]], axis=1))
        else:
            w_in = w_in_c[j].astype(BF16)
            w_out = w_out_c[j].astype(BF16)
            xq_cb = 3 * CONF_WIDTH // X_WIDTH

            proj = _norm_matmul(xp, norm_c[j], w_in, tm_p, 1792)
            y_conf, tail = _conf_prompt(proj, conf_dw_w[j], conf_dw_b[j], conf_ln_g[j], conf_ln_b[j], bp, sp, tt_conf)
            y_x = _xattn_prompt(proj, xq_cb, mem_kv3, layer, bp, sp, tq_x)
            xp = _out_proj([y_conf, y_x], w_out, xp, norm_f, final, tm_out)
            pconf.append(tail[:, CONF_PAD - (CONF_CONV_W - 1):])

            proj = _norm_matmul(xs, norm_c[j], w_in, bs, 1024)
            st_t = jnp.swapaxes(state_conf_conv[j], 0, 1)
            y_conf, u = _conf_sample(proj, st_t, conf_dw_w[j], conf_dw_b[j], conf_ln_g[j], conf_ln_b[j], 256)
            y_x = _xattn_sample(proj, xq_cb, mem_k_rows, mem_v_rows, layer, rb_x)
            xs = _out_proj([y_conf, y_x], w_out, xs, norm_f, final, bs)
            sconf.append(jnp.concatenate([state_conf_conv[j][:, 1:], u[:, None]], axis=1))

    p_attn_k = jnp.transpose(jnp.stack(pk).reshape(n_a, bp, DIFF_HEADS, 2, DIFF_DK, sp), (0, 1, 5, 2, 3, 4))
    return (xp.reshape(bp, sp, d), xs.reshape(bs, 1, d), p_attn_k, jnp.stack(pv), jnp.stack(pssm),
            jnp.stack(pconv), jnp.stack(pconf), p_mem_k, p_mem_v, jnp.stack(sk), jnp.stack(sv), jnp.stack(sssm),
            jnp.stack(sconv), jnp.stack(sconf))
```
